```python
import jax, jax.numpy as jnp
from jax import lax
import numpy as np

D_MODEL = 1024
BATCH = 1
SEQ = 16384
DEPTH = 1

PLE_DIM = 256
EPS = 1e-6
RET_HEADS = 4
RET_QK_DIM = 256
RET_V_DIM = 512
RET_CHUNK = 128
RET_ROPE_BASE = 10000.0
DIL_GROUPS = ((128, 1), (512, 4), (2048, 16))
DIL_SLOTS = 8
DIL_HEAD_DIM = 64
DIL_BLOCK = 128
ROPE_THETA = 500000.0
ROPE_DIM = DIL_HEAD_DIM // 4
D_FF = 4 * D_MODEL

RET_QK_W = RET_HEADS * RET_QK_DIM
RET_V_W = RET_HEADS * RET_V_DIM
DIL_HEADS = len(DIL_GROUPS) * DIL_SLOTS
DIL_W = DIL_HEADS * DIL_HEAD_DIM
DIL_OUT_W = DIL_SLOTS * DIL_HEAD_DIM
SPLITS = (RET_QK_W, RET_QK_W, RET_V_W, RET_V_W, DIL_W, DIL_W, DIL_W, D_MODEL, D_MODEL)
IN_W = RET_QK_W * 2 + RET_V_W * 2 + DIL_W * 3 + D_MODEL * 2

kernel_name = "hybrid_retention_dilated_attn_block"


def rmsnorm(x, g=None):
    xf = x.astype(jnp.float32)
    y = xf * lax.rsqrt(jnp.mean(xf * xf, axis=-1, keepdims=True) + EPS)
    if g is not None:
        y = y * g.astype(jnp.float32)
    return y.astype(x.dtype)


def rope(x, cos, sin, rot_dim):
    half = rot_dim // 2
    x1 = x[..., :half]
    x2 = x[..., half:rot_dim]
    return jnp.concatenate([x1 * cos - x2 * sin, x2 * cos + x1 * sin, x[..., rot_dim:]], axis=-1)


def retention(q, k, v, pos):
    B, S, H, dk = q.shape
    dv = v.shape[-1]
    f32 = jnp.float32
    half = dk // 2
    inv_freq = 1.0 / (RET_ROPE_BASE ** jnp.linspace(0.0, 1.0, half, dtype=f32))
    ang = pos.astype(f32)[:, :, None, None] * inv_freq
    cos, sin = jnp.cos(ang), jnp.sin(ang)
    q = rope(q.astype(f32), cos, sin, dk)
    k = rope(k.astype(f32), cos, sin, dk) * (dk ** -0.5)
    v = v.astype(f32)
    C = RET_CHUNK
    Sp = -(-S // C) * C
    padw = ((0, 0), (0, Sp - S), (0, 0), (0, 0))
    nc = Sp // C
    qc = jnp.pad(q, padw).reshape(B, nc, C, H, dk)
    kc = jnp.pad(k, padw).reshape(B, nc, C, H, dk)
    vc = jnp.pad(v, padw).reshape(B, nc, C, H, dv)
    log_gamma = jnp.log1p(-(2.0 ** (-5.0 - jnp.arange(H, dtype=f32))))
    idx = jnp.arange(C, dtype=f32)
    diff = idx[:, None] - idx[None, :]
    decay = jnp.where(diff[None] >= 0, jnp.exp(jnp.maximum(diff, 0.0)[None] * log_gamma[:, None, None]), 0.0)
    scores = jnp.einsum('bnihd,bnjhd->bnhij', qc, kc) * decay[None, None]
    inner = jnp.einsum('bnhij,bnjhe->bnihe', scores, vc)
    q_dec = jnp.exp((idx + 1.0)[:, None] * log_gamma[None, :])
    k_dec = jnp.exp((C - 1.0 - idx)[:, None] * log_gamma[None, :])
    chunk_dec = jnp.exp(C * log_gamma)

    def step(R, inp):
        qn, kn, vn = inp
        cross = jnp.einsum('bihd,bhde->bihe', qn * q_dec[None, :, :, None], R)
        R = R * chunk_dec[None, :, None, None] + jnp.einsum('bjhd,bjhe->bhde', kn * k_dec[None, :, :, None], vn)
        return R, cross

    R0 = jnp.zeros((B, H, dk, dv), f32)
    _, cross = lax.scan(step, R0, (jnp.moveaxis(qc, 1, 0), jnp.moveaxis(kc, 1, 0), jnp.moveaxis(vc, 1, 0)))
    y = inner + jnp.moveaxis(cross, 0, 1)
    return y.reshape(B, Sp, H, dv)[:, :S]


def dilated_group(q, k, v, window, dilation):
    B, S, Hg, hd = q.shape
    f32 = jnp.float32
    band = window // dilation
    QB = DIL_BLOCK
    seg = dilation * QB
    Sp = -(-S // seg) * seg
    L = Sp // dilation
    nb = L // QB
    padw = ((0, 0), (0, Sp - S), (0, 0), (0, 0))

    def to_streams(t):
        t = jnp.pad(t.astype(f32), padw).reshape(B, L, dilation, Hg, hd)
        return t.transpose(0, 2, 1, 3, 4).reshape(B, dilation, nb, QB, Hg, hd)

    def with_prev(t):
        prev = jnp.pad(t[:, :, :-1], ((0, 0), (0, 0), (1, 0), (0, 0), (0, 0), (0, 0)))
        return jnp.concatenate([prev, t], axis=3)

    qs = to_streams(q)
    kk = with_prev(to_streams(k))
    vv = with_prev(to_streams(v))
    s = jnp.einsum('bcnqhe,bcnkhe->bcnhqk', qs, kk) * (hd ** -0.5)
    qi = jnp.arange(QB)[:, None]
    kj = jnp.arange(2 * QB)[None, :] - QB
    dist = qi - kj
    in_band = (dist >= 0) & (dist <= band)
    key_exists = (jnp.arange(nb)[:, None, None] * QB + kj[None]) >= 0
    mask = in_band[None] & key_exists
    s = jnp.where(mask[None, None, :, None], s, -1e30)
    lse = jax.nn.logsumexp(s, axis=-1)
    pr = jnp.exp(s - lse[..., None])
    o = jnp.einsum('bcnhqk,bcnkhe->bcnqhe', pr, vv)
    o = o.reshape(B, dilation, L, Hg, hd).transpose(0, 2, 1, 3, 4).reshape(B, Sp, Hg, hd)[:, :S]
    lse = lse.transpose(0, 1, 2, 4, 3).reshape(B, dilation, L, Hg).transpose(0, 2, 1, 3).reshape(B, Sp, Hg)[:, :S]
    return o, lse


def dilated_attention(q, k, v, pos):
    f32 = jnp.float32
    freqs = ROPE_THETA ** (-jnp.arange(0, ROPE_DIM, 2, dtype=f32) / ROPE_DIM)
    ang = pos.astype(f32)[:, :, None, None] * freqs
    cos, sin = jnp.cos(ang), jnp.sin(ang)
    q = rope(q.astype(f32), cos, sin, ROPE_DIM)
    k = rope(k.astype(f32), cos, sin, ROPE_DIM)
    outs, lses = [], []
    for g, (window, dilation) in enumerate(DIL_GROUPS):
        sl = slice(g * DIL_SLOTS, (g + 1) * DIL_SLOTS)
        o, l = dilated_group(q[:, :, sl], k[:, :, sl], v[:, :, sl], window, dilation)
        outs.append(o)
        lses.append(l)
    w = jax.nn.softmax(jnp.stack(lses, axis=0), axis=0)
    return jnp.sum(w[..., None] * jnp.stack(outs, axis=0), axis=0)


def setup_inputs(seed: int = 0) -> dict:
    key = jax.random.key(seed)
    ks = jax.random.split(key, 24)
    f32 = jnp.float32
    nrm = lambda k, shape, fan_in: jax.random.normal(k, shape, f32) * (fan_in ** -0.5)
    gain = lambda k: 1.0 + 0.02 * jax.random.normal(k, (DEPTH, D_MODEL), f32)
    x = jax.random.normal(ks[0], (BATCH, SEQ, D_MODEL), f32)
    p = jax.random.normal(ks[1], (DEPTH, BATCH, SEQ, PLE_DIM), f32)
    positions = jnp.broadcast_to(jnp.arange(SEQ, dtype=jnp.int32)[None, :], (BATCH, SEQ))
    return {
        "x": x,
        "p": p,
        "positions": positions,
        "w_in": nrm(ks[2], (DEPTH, D_MODEL, IN_W), D_MODEL),
        "b_gate": 0.01 * jax.random.normal(ks[3], (DEPTH, 2, D_MODEL), f32),
        "w_ret_out": nrm(ks[4], (DEPTH, RET_V_W, D_MODEL), RET_V_W),
        "w_dil_out": nrm(ks[5], (DEPTH, DIL_OUT_W, D_MODEL), DIL_OUT_W),
        "w_o": nrm(ks[6], (DEPTH, D_MODEL, D_MODEL), D_MODEL),
        "g_pre_mix": gain(ks[7]),
        "g_post_mix": gain(ks[8]),
        "g_pre_mlp": gain(ks[9]),
        "g_post_mlp": gain(ks[10]),
        "w_up": nrm(ks[11], (DEPTH, D_MODEL, D_FF), D_MODEL),
        "w_down": nrm(ks[12], (DEPTH, D_FF, D_MODEL), D_FF),
        "g_pre_ple": gain(ks[13]),
        "w_ple_gate": nrm(ks[14], (DEPTH, D_MODEL, D_MODEL), D_MODEL),
        "b_ple_gate": 0.01 * jax.random.normal(ks[15], (DEPTH, D_MODEL), f32),
        "w_ple_in": nrm(ks[16], (DEPTH, PLE_DIM, D_MODEL), PLE_DIM),
        "g_post_ple": gain(ks[17]),
    }


def reference(x, p, positions, w_in, b_gate, w_ret_out, w_dil_out, w_o, g_pre_mix, g_post_mix,
              g_pre_mlp, g_post_mlp, w_up, w_down, g_pre_ple, w_ple_gate, b_ple_gate, w_ple_in,
              g_post_ple):
    B, S, _ = x.shape
    split_points = [int(s) for s in np.cumsum(SPLITS)[:-1]]
    h = x
    for i in range(DEPTH):
        u = rmsnorm(h, g_pre_mix[i])
        proj = u @ w_in[i]
        rq, rk, rv, rg, aq, ak, av, gr, ga = jnp.split(proj, split_points, axis=-1)
        yr = retention(rq.reshape(B, S, RET_HEADS, RET_QK_DIM), rk.reshape(B, S, RET_HEADS, RET_QK_DIM),
                       rv.reshape(B, S, RET_HEADS, RET_V_DIM), positions)
        yr = rmsnorm(yr) * jax.nn.silu(rg.reshape(B, S, RET_HEADS, RET_V_DIM).astype(jnp.float32))
        ya_branch = yr.reshape(B, S, RET_V_W).astype(h.dtype) @ w_ret_out[i]
        ya = dilated_attention(aq.reshape(B, S, DIL_HEADS, DIL_HEAD_DIM), ak.reshape(B, S, DIL_HEADS, DIL_HEAD_DIM),
                               av.reshape(B, S, DIL_HEADS, DIL_HEAD_DIM), positions)
        yb_branch = ya.reshape(B, S, DIL_OUT_W).astype(h.dtype) @ w_dil_out[i]
        mixed = jax.nn.sigmoid(gr + b_gate[i, 0]) * ya_branch + jax.nn.sigmoid(ga + b_gate[i, 1]) * yb_branch
        h = h + rmsnorm(mixed @ w_o[i], g_post_mix[i])
        v2 = rmsnorm(h, g_pre_mlp[i])
        f = jnp.square(jax.nn.relu(v2 @ w_up[i])) @ w_down[i]
        h = h + rmsnorm(f, g_post_mlp[i])
        gate = jax.nn.sigmoid(rmsnorm(h, g_pre_ple[i]) @ w_ple_gate[i] + b_ple_gate[i])
        e = p[i].astype(h.dtype) @ w_ple_in[i]
        h = h + rmsnorm(gate * e, g_post_ple[i])
    return h
```

```python
import functools
import math

import numpy as np
import jax
import jax.numpy as jnp
from jax import lax
from jax.experimental import pallas as pl
from jax.experimental.pallas import tpu as pltpu

F32 = jnp.float32
BF16 = jnp.bfloat16

D_MODEL = 1024
PLE_DIM = 256
EPS = 1e-6
RET_HEADS = 4
RET_QK_DIM = 256
RET_V_DIM = 512
RET_ROPE_BASE = 10000.0
DIL_GROUPS = ((128, 1), (512, 4), (2048, 16))
DIL_SLOTS = 8
DIL_HEAD_DIM = 64
DIL_BLOCK = 128
ROPE_THETA = 500000.0
ROPE_DIM = DIL_HEAD_DIM // 4
D_FF = 4 * D_MODEL
RET_QK_W = RET_HEADS * RET_QK_DIM
RET_V_W = RET_HEADS * RET_V_DIM
DIL_OUT_W = DIL_SLOTS * DIL_HEAD_DIM
DIL_W = len(DIL_GROUPS) * DIL_OUT_W
IN_W = RET_QK_W * 2 + RET_V_W * 2 + DIL_W * 3 + D_MODEL * 2
OFF_RQ, OFF_RK, OFF_RV, OFF_RG = 0, RET_QK_W, 2 * RET_QK_W, 2 * RET_QK_W + RET_V_W
OFF_AQ = 2 * RET_QK_W + 2 * RET_V_W
OFF_AK, OFF_AV = OFF_AQ + DIL_W, OFF_AQ + 2 * DIL_W
OFF_GR = OFF_AQ + 3 * DIL_W
OFF_GA = OFF_GR + D_MODEL

LANES = 128
NEG_INF = -1e30
MIB = 1024 * 1024

INPROJ_TM = 1024
INPROJ_TN = 1280
RET_CHUNK = 256
POST_TM = 256
FF_CHUNK = 1024
LSE_LANES = LANES // DIL_SLOTS


def _rms(x):
    return x * lax.rsqrt(jnp.mean(x * x, axis=-1, keepdims=True) + EPS)


def _bdot(a, b):
    return jnp.dot(a, b, preferred_element_type=F32)


def _inproj_kernel(x_ref, g_ref, w_ref, o_ref, u_ref):
    @pl.when(pl.program_id(1) == 0)
    def _():
        u_ref[...] = (_rms(x_ref[...]) * g_ref[...]).astype(BF16)

    o_ref[...] = _bdot(u_ref[...], w_ref[...]).astype(BF16)


def _inproj(x, g, w):
    S = x.shape[0]
    tm, tn = INPROJ_TM, INPROJ_TN
    return pl.pallas_call(
        _inproj_kernel,
        grid=(S // tm, IN_W // tn),
        in_specs=[
            pl.BlockSpec((tm, D_MODEL), lambda i, j: (i, 0)),
            pl.BlockSpec((1, D_MODEL), lambda i, j: (0, 0)),
            pl.BlockSpec((D_MODEL, tn), lambda i, j: (0, j)),
        ],
        out_specs=pl.BlockSpec((tm, tn), lambda i, j: (i, j)),
        out_shape=jax.ShapeDtypeStruct((S, IN_W), BF16),
        scratch_shapes=[pltpu.VMEM((tm, D_MODEL), BF16)],
        compiler_params=pltpu.CompilerParams(
            dimension_semantics=("parallel", "arbitrary"),
            vmem_limit_bytes=40 * MIB),
        name="inproj",
    )(x, g, w)


def _tab_kernel(pos_ref, f_ref, cos_ref, sin_ref):
    ang = f_ref[...] * pos_ref[...]
    cos_ref[...] = jnp.cos(ang)
    sin_ref[...] = jnp.sin(ang)


def _rope_tables(pos_row, freqs_col):
    S = pos_row.shape[1]
    n = freqs_col.shape[0]
    return pl.pallas_call(
        _tab_kernel,
        out_shape=(jax.ShapeDtypeStruct((n, S), F32), jax.ShapeDtypeStruct((n, S), F32)),
        name="rope_tab",
    )(pos_row, freqs_col)


def _ret_kernel(pos_ref, invf_ref, dec_ref, qdec_ref, kdec_ref, q_ref, k_ref, v_ref, g_ref,
                o_ref, r_ref, *, chunk_dec):
    @pl.when(pl.program_id(0) == 0)
    def _():
        r_ref[...] = jnp.zeros_like(r_ref)

    half = RET_QK_DIM // 2
    ang = pos_ref[...] * invf_ref[...]
    cos = jnp.cos(ang)
    sin = jnp.sin(ang)
    for h in range(RET_HEADS):
        qo = h * RET_QK_DIM
        q1 = q_ref[:, qo:qo + half].astype(F32)
        q2 = q_ref[:, qo + half:qo + RET_QK_DIM].astype(F32)
        k1 = k_ref[:, qo:qo + half].astype(F32)
        k2 = k_ref[:, qo + half:qo + RET_QK_DIM].astype(F32)
        qr1 = q1 * cos - q2 * sin
        qr2 = q2 * cos + q1 * sin
        kr1 = k1 * cos - k2 * sin
        kr2 = k2 * cos + k1 * sin
        qdec = qdec_ref[h]
        kdec = kdec_ref[h]
        scale = RET_QK_DIM ** -0.5
        qr = jnp.concatenate([qr1, qr2], axis=-1).astype(BF16)
        qd = jnp.concatenate([qr1 * qdec, qr2 * qdec], axis=-1).astype(BF16)
        kr = jnp.concatenate([kr1 * scale, kr2 * scale], axis=-1).astype(BF16)
        kd = jnp.concatenate([kr1 * (kdec * scale), kr2 * (kdec * scale)], axis=-1).astype(BF16)
        vo = h * RET_V_DIM
        v = v_ref[:, vo:vo + RET_V_DIM]
        s = lax.dot_general(qr, kr, (((1,), (1,)), ((), ())), preferred_element_type=F32)
        s = s * dec_ref[h]
        inner = _bdot(s.astype(BF16), v)
        r = r_ref[h]
        cross = _bdot(qd, r.astype(BF16))
        upd = lax.dot_general(kd, v, (((0,), (0,)), ((), ())), preferred_element_type=F32)
        r_ref[h] = r * chunk_dec[h] + upd
        y = _rms(inner + cross)
        g = g_ref[:, vo:vo + RET_V_DIM].astype(F32)
        o_ref[:, vo:vo + RET_V_DIM] = (y * (g * jax.nn.sigmoid(g))).astype(BF16)


def _retention(proj, pos_col):
    S = proj.shape[0]
    C = RET_CHUNK
    half = RET_QK_DIM // 2
    H = RET_HEADS
    log_gamma = np.log1p(-(2.0 ** (-5.0 - np.arange(H, dtype=np.float64))))
    idx = np.arange(C, dtype=np.float64)
    diff = idx[:, None] - idx[None, :]
    dec = np.where(diff[None] >= 0, np.exp(np.maximum(diff, 0.0)[None] * log_gamma[:, None, None]), 0.0)
    qdec = np.exp((idx + 1.0)[None, :] * log_gamma[:, None])
    kdec = np.exp((C - 1.0 - idx)[None, :] * log_gamma[:, None])
    chunk_dec = tuple(float(v) for v in np.exp(C * log_gamma))
    qdec_b = np.broadcast_to(qdec[:, :, None], (H, C, half))
    kdec_b = np.broadcast_to(kdec[:, :, None], (H, C, half))
    inv_freq = (1.0 / (RET_ROPE_BASE ** jnp.linspace(0.0, 1.0, half, dtype=F32))).reshape(1, half)
    const = lambda shape: pl.BlockSpec(shape, lambda n: (0,) * len(shape))
    return pl.pallas_call(
        functools.partial(_ret_kernel, chunk_dec=chunk_dec),
        grid=(S // C,),
        in_specs=[
            pl.BlockSpec((C, 1), lambda n: (n, 0)),
            const((1, half)),
            const((H, C, C)),
            const((H, C, half)),
            const((H, C, half)),
            pl.BlockSpec((C, RET_QK_W), lambda n: (n, OFF_RQ // RET_QK_W)),
            pl.BlockSpec((C, RET_QK_W), lambda n: (n, OFF_RK // RET_QK_W)),
            pl.BlockSpec((C, RET_V_W), lambda n: (n, OFF_RV // RET_V_W)),
            pl.BlockSpec((C, RET_V_W), lambda n: (n, OFF_RG // RET_V_W)),
        ],
        out_specs=pl.BlockSpec((C, RET_V_W), lambda n: (n, 0)),
        out_shape=jax.ShapeDtypeStruct((S, RET_V_W), BF16),
        scratch_shapes=[pltpu.VMEM((H, RET_QK_DIM, RET_V_DIM), F32)],
        compiler_params=pltpu.CompilerParams(
            dimension_semantics=("arbitrary",), vmem_limit_bytes=40 * MIB),
        name="retention",
    )(pos_col, inv_freq, jnp.asarray(dec, F32), jnp.asarray(qdec_b, F32), jnp.asarray(kdec_b, F32),
      proj, proj, proj, proj)


def _dil_kernel(cos_ref, sinp_ref, sinm_ref, band_ref, q_ref, k_ref, v_ref, o_ref, lse_ref,
                kprev_ref, vprev_ref):
    n = pl.program_id(1)
    QB = DIL_BLOCK

    @pl.when(n == 0)
    def _():
        kprev_ref[...] = jnp.zeros_like(kprev_ref)
        vprev_ref[...] = jnp.zeros_like(vprev_ref)

    cos = cos_ref[...]
    sinp = sinp_ref[...]
    sinm = sinm_ref[...]

    def rope(ref, scale):
        cols = []
        for c in range(DIL_OUT_W // LANES):
            x = ref[:, c * LANES:(c + 1) * LANES].astype(F32)
            y = x * cos + pltpu.roll(x, ROPE_DIM // 2, 1) * sinp + pltpu.roll(x, LANES - ROPE_DIM // 2, 1) * sinm
            cols.append((y * scale).astype(BF16))
        return cols

    q_cols = rope(q_ref, DIL_HEAD_DIM ** -0.5)
    k_cols = rope(k_ref, 1.0)

    col = lax.broadcasted_iota(jnp.int32, (QB, 2 * QB), 1)
    bias = jnp.where(jnp.logical_or(col >= QB, n > 0), band_ref[...], NEG_INF)
    lane = lax.broadcasted_iota(jnp.int32, (QB, LANES), 1)
    lse_tile = jnp.zeros((QB, LANES), F32)
    for pr in range(DIL_OUT_W // LANES):
        sl = slice(pr * LANES, (pr + 1) * LANES)
        kp = jnp.concatenate([kprev_ref[:, sl], k_cols[pr]], axis=0)
        vp = jnp.concatenate([vprev_ref[:, sl], v_ref[:, sl]], axis=0)
        outs = []
        for hh in range(2):
            in_head = (lane >= hh * DIL_HEAD_DIM) & (lane < (hh + 1) * DIL_HEAD_DIM)
            qm = jnp.where(in_head, q_cols[pr], jnp.zeros_like(q_cols[pr]))
            s = lax.dot_general(qm, kp, (((1,), (1,)), ((), ())), preferred_element_type=F32) + bias
            m = jnp.max(s, axis=-1, keepdims=True)
            p = jnp.exp(s - m)
            l = jnp.sum(p, axis=-1, keepdims=True)
            o = _bdot(p.astype(BF16), vp) / l
            outs.append(o)
            lse = m + jnp.log(l)
            head = 2 * pr + hh
            lse_tile = jnp.where((lane >= head * LSE_LANES) & (lane < (head + 1) * LSE_LANES), lse, lse_tile)
        o_ref[:, sl] = jnp.where(lane < DIL_HEAD_DIM, outs[0], outs[1]).astype(BF16)
        kprev_ref[:, sl] = k_cols[pr]
        vprev_ref[:, sl] = v_ref[:, sl]
    lse_ref[...] = lse_tile


def _dil_group(proj, tabs, band_bias, g, dilation):
    S = proj.shape[0]
    d = dilation
    L = S // d
    nb = L // DIL_BLOCK
    QB = DIL_BLOCK
    W = DIL_OUT_W
    ncol = IN_W // W
    pv = proj.reshape(L, d * IN_W)
    tv = [t.reshape(L, d * LANES) for t in tabs]
    tab_spec = pl.BlockSpec((QB, LANES), lambda c, n: (n, c))
    sec = lambda off: pl.BlockSpec((QB, W), lambda c, n, off=off: (n, c * ncol + off // W + g))
    o, lse = pl.pallas_call(
        _dil_kernel,
        grid=(d, nb),
        in_specs=[tab_spec, tab_spec, tab_spec,
                  pl.BlockSpec((QB, 2 * QB), lambda c, n: (0, 0)),
                  sec(OFF_AQ), sec(OFF_AK), sec(OFF_AV)],
        out_specs=[pl.BlockSpec((QB, W), lambda c, n: (n, c)),
                   pl.BlockSpec((QB, LANES), lambda c, n: (n, c))],
        out_shape=[jax.ShapeDtypeStruct((L, d * W), BF16),
                   jax.ShapeDtypeStruct((L, d * LANES), F32)],
        scratch_shapes=[pltpu.VMEM((QB, W), BF16), pltpu.VMEM((QB, W), BF16)],
        compiler_params=pltpu.CompilerParams(
            dimension_semantics=("arbitrary", "arbitrary"), vmem_limit_bytes=32 * MIB),
        name=f"dil_attn_{g}",
    )(*tv, band_bias, pv, pv, pv)
    return o.reshape(S, W), lse.reshape(S, LANES)


def _post_kernel(x_ref, yr_ref, o1_ref, o2_ref, o3_ref, l1_ref, l2_ref, l3_ref,
                 gr0_ref, gr1_ref, ga0_ref, ga1_ref, p_ref, e_ref,
                 bg_ref, wro_ref, wdo_ref, wo_ref, gpm_ref, gprm_ref, wup_ref, wdn_ref, gpom_ref,
                 gprp_ref, wpg_ref, bpg_ref, wpi_ref, gpop_ref, out_ref):
    l1, l2, l3 = l1_ref[...], l2_ref[...], l3_ref[...]
    mx = jnp.maximum(jnp.maximum(l1, l2), l3)
    e1, e2, e3 = jnp.exp(l1 - mx), jnp.exp(l2 - mx), jnp.exp(l3 - mx)
    den = e1 + e2 + e3
    expand = e_ref[...]

    def widen(w):
        hi = w.astype(BF16)
        lo = (w - hi.astype(F32)).astype(BF16)
        return _bdot(hi, expand) + _bdot(lo, expand)

    merged = (widen(e1 / den) * o1_ref[...].astype(F32)
              + widen(e2 / den) * o2_ref[...].astype(F32)
              + widen(e3 / den) * o3_ref[...].astype(F32))
    yb = _bdot(merged.astype(BF16), wdo_ref[...])
    ya = _bdot(yr_ref[...], wro_ref[...])
    gr = jnp.concatenate([gr0_ref[...], gr1_ref[...]], axis=-1).astype(F32) + bg_ref[0:1, :]
    ga = jnp.concatenate([ga0_ref[...], ga1_ref[...]], axis=-1).astype(F32) + bg_ref[1:2, :]
    mixed = jax.nn.sigmoid(gr) * ya + jax.nn.sigmoid(ga) * yb
    h = x_ref[...] + _rms(_bdot(mixed.astype(BF16), wo_ref[...])) * gpm_ref[...]
    v2 = (_rms(h) * gprm_ref[...]).astype(BF16)
    f = jnp.zeros(h.shape, F32)
    for c in range(D_FF // FF_CHUNK):
        a = jnp.maximum(_bdot(v2, wup_ref[:, c * FF_CHUNK:(c + 1) * FF_CHUNK]), 0.0)
        f = f + _bdot((a * a).astype(BF16), wdn_ref[c * FF_CHUNK:(c + 1) * FF_CHUNK, :])
    h = h + _rms(f) * gpom_ref[...]
    gate = jax.nn.sigmoid(_bdot((_rms(h) * gprp_ref[...]).astype(BF16), wpg_ref[...]) + bpg_ref[...])
    e = _bdot(p_ref[...].astype(BF16), wpi_ref[...])
    out_ref[...] = h + _rms(gate * e) * gpop_ref[...]


def _post(x, yr, os_, lses, proj, p, expand, b_gate, w_ret_out, w_dil_out, w_o, g_post_mix, g_pre_mlp,
          w_up, w_down, g_post_mlp, g_pre_ple, w_ple_gate, b_ple_gate, w_ple_in, g_post_ple):
    S = x.shape[0]
    tm = POST_TM
    hw = D_MODEL // 2
    row = lambda w: pl.BlockSpec((tm, w), lambda i: (i, 0))
    gate_blk = lambda off: pl.BlockSpec((tm, hw), lambda i, off=off: (i, off // hw))
    res = lambda a: pl.BlockSpec(a.shape, lambda i: (0,) * a.ndim, pipeline_mode=pl.Buffered(1))
    weights = (b_gate, w_ret_out, w_dil_out, w_o, g_post_mix, g_pre_mlp, w_up, w_down, g_post_mlp,
               g_pre_ple, w_ple_gate, b_ple_gate, w_ple_in, g_post_ple)
    return pl.pallas_call(
        _post_kernel,
        grid=(S // tm,),
        in_specs=[row(D_MODEL), row(RET_V_W), row(DIL_OUT_W), row(DIL_OUT_W), row(DIL_OUT_W),
                  row(LANES), row(LANES), row(LANES),
                  gate_blk(OFF_GR), gate_blk(OFF_GR + hw), gate_blk(OFF_GA), gate_blk(OFF_GA + hw),
                  row(PLE_DIM), res(expand)] + [res(w) for w in weights],
        out_specs=row(D_MODEL),
        out_shape=jax.ShapeDtypeStruct((S, D_MODEL), F32),
        compiler_params=pltpu.CompilerParams(
            dimension_semantics=("parallel",), vmem_limit_bytes=56 * MIB),
        name="post",
    )(x, yr, *os_, *lses, proj, proj, proj, proj, p, expand, *weights)


def _dil_constants():
    QB = DIL_BLOCK
    qi = np.arange(QB)[:, None]
    kj = np.arange(2 * QB)[None, :] - QB
    dist = qi - kj
    band = np.where((dist >= 0) & (dist <= QB), 0.0, NEG_INF).astype(np.float32)
    expand = np.zeros((LANES, DIL_OUT_W), np.float32)
    for h in range(DIL_SLOTS):
        expand[h * LSE_LANES, h * DIL_HEAD_DIM:(h + 1) * DIL_HEAD_DIM] = 1.0
    return jnp.asarray(band), jnp.asarray(expand, BF16)


def kernel(x, p, positions, w_in, b_gate, w_ret_out, w_dil_out, w_o, g_pre_mix, g_post_mix, g_pre_mlp,
           g_post_mlp, w_up, w_down, g_pre_ple, w_ple_gate, b_ple_gate, w_ple_in, g_post_ple):
    B, S, _ = x.shape
    assert B == 1 and w_in.shape[0] == 1
    for window, dilation in DIL_GROUPS:
        assert window // dilation == DIL_BLOCK and S % (dilation * DIL_BLOCK) == 0
    xs = x[0]
    pos = positions[0].astype(F32)
    bf = lambda a: a[0].astype(BF16)
    vec = lambda a: a[0].reshape(1, -1)

    proj = _inproj(xs, vec(g_pre_mix), bf(w_in))
    yr = _retention(proj, pos.reshape(S, 1))

    freqs = ROPE_THETA ** (-jnp.arange(0, ROPE_DIM, 2, dtype=F32) / ROPE_DIM)
    cos8, sin8 = _rope_tables(pos.reshape(1, S), freqs.reshape(-1, 1))
    cos8, sin8 = cos8.T, sin8.T
    hr = ROPE_DIM // 2
    rest = DIL_HEAD_DIM - ROPE_DIM
    per_head = lambda parts: jnp.tile(jnp.concatenate(parts, axis=1), (1, LANES // DIL_HEAD_DIM))
    cos_t = per_head([cos8, cos8, jnp.ones((S, rest), F32)])
    sinp_t = per_head([jnp.zeros((S, hr), F32), sin8, jnp.zeros((S, rest), F32)])
    sinm_t = per_head([-sin8, jnp.zeros((S, hr + rest), F32)])
    band_bias, expand = _dil_constants()

    os_, lses = [], []
    for g, (_, dilation) in enumerate(DIL_GROUPS):
        o, lse = _dil_group(proj, (cos_t, sinp_t, sinm_t), band_bias, g, dilation)
        os_.append(o)
        lses.append(lse)

    out = _post(xs, yr, os_, lses, proj, p[0, 0], expand, b_gate[0], bf(w_ret_out), bf(w_dil_out), bf(w_o),
                vec(g_post_mix), vec(g_pre_mlp), bf(w_up), bf(w_down), vec(g_post_mlp), vec(g_pre_ple),
                bf(w_ple_gate), vec(b_ple_gate), bf(w_ple_in), vec(g_post_ple))
    return out[None]
```

```python
import functools

import numpy as np
import jax
import jax.numpy as jnp
from jax import lax
from jax.experimental import pallas as pl
from jax.experimental.pallas import tpu as pltpu

F32 = jnp.float32
BF16 = jnp.bfloat16

D_MODEL = 1024
PLE_DIM = 256
EPS = 1e-6
RET_HEADS = 4
RET_QK_DIM = 256
RET_V_DIM = 512
RET_ROPE_BASE = 10000.0
DIL_GROUPS = ((128, 1), (512, 4), (2048, 16))
DIL_SLOTS = 8
DIL_HEAD_DIM = 64
DIL_BLOCK = 128
ROPE_THETA = 500000.0
ROPE_DIM = DIL_HEAD_DIM // 4
D_FF = 4 * D_MODEL
RET_QK_W = RET_HEADS * RET_QK_DIM
RET_V_W = RET_HEADS * RET_V_DIM
RET_W = 2 * RET_QK_W + 2 * RET_V_W
DIL_OUT_W = DIL_SLOTS * DIL_HEAD_DIM
DIL_W = len(DIL_GROUPS) * DIL_OUT_W
GATE_W = 2 * D_MODEL
IN_W = RET_W + DIL_W * 3 + GATE_W
OFF_DIL = RET_W
OFF_GATE = RET_W + 3 * DIL_W

LANES = 128
NEG_INF = -1e30
MIB = 1024 * 1024

INPROJ_TM = 256
INPROJ_TN = 512
RET_CHUNK = 256
POST_TM = 256
FF_CHUNK = 1024
LSE_LANES = LANES // DIL_SLOTS


def _rms(x):
    return x * lax.rsqrt(jnp.mean(x * x, axis=-1, keepdims=True) + EPS)


def _bdot(a, b):
    return jnp.dot(a, b, preferred_element_type=F32)


def _resident(a):
    return pl.BlockSpec(a.shape, lambda *_: (0,) * a.ndim, pipeline_mode=pl.Buffered(1))


def _inproj_kernel(x_ref, pos_ref, f_ref, g_ref, w_ref, ret_ref, a0_ref, a1_ref, a2_ref, gate_ref,
                   u_ref, scr_ref):
    tm = x_ref.shape[0]
    tn = INPROJ_TN
    u_ref[...] = (_rms(x_ref[...]) * g_ref[...]).astype(BF16)

    ang = f_ref[...] * pos_ref[...]
    c8, s8 = jnp.cos(ang), jnp.sin(ang)
    hr = ROPE_DIM // 2
    rest = DIL_HEAD_DIM - ROPE_DIM
    reps = LANES // DIL_HEAD_DIM
    one, zr, z8 = jnp.ones((rest, tm), F32), jnp.zeros((rest, tm), F32), jnp.zeros((hr, tm), F32)
    cos_t = jnp.concatenate([c8, c8, one] * reps, axis=0).T
    sinp_t = jnp.concatenate([z8, s8, zr] * reps, axis=0).T
    sinm_t = jnp.concatenate([-s8, z8, zr] * reps, axis=0).T

    def rope(y, scale):
        cols = []
        for c in range(y.shape[1] // LANES):
            v = y[:, c * LANES:(c + 1) * LANES]
            r = v * cos_t + pltpu.roll(v, hr, 1) * sinp_t + pltpu.roll(v, LANES - hr, 1) * sinm_t
            cols.append(r * scale if scale != 1.0 else r)
        return jnp.concatenate(cols, axis=-1)

    def proj(col):
        return _bdot(u_ref[...], w_ref[:, col:col + tn])

    for j in range(RET_W // tn):
        ret_ref[:, j * tn:(j + 1) * tn] = proj(j * tn).astype(BF16)
    a_refs = (a0_ref, a1_ref, a2_ref)
    for which in range(3):
        for g, (_, d) in enumerate(DIL_GROUPS):
            y = proj(OFF_DIL + which * DIL_W + g * DIL_OUT_W)
            if which == 0:
                y = rope(y, DIL_HEAD_DIM ** -0.5)
            elif which == 1:
                y = rope(y, 1.0)
            if d == 1:
                a_refs[g][which, 0] = y.astype(BF16)
            else:
                ncol = DIL_OUT_W // LANES
                s0 = (which * len(DIL_GROUPS) + g) * ncol
                for cc in range(ncol):
                    scr_ref[s0 + cc] = y[:, cc * LANES:(cc + 1) * LANES]
                for c in range(d):
                    a_refs[g][which, c] = jnp.concatenate(
                        [scr_ref[s0 + cc, pl.ds(c, tm // d, stride=d), :] for cc in range(ncol)],
                        axis=-1).astype(BF16)
    for j in range(GATE_W // tn):
        gate_ref[:, j * tn:(j + 1) * tn] = proj(OFF_GATE + j * tn).astype(BF16)


def _inproj(x, pos_row, freqs_col, g, w):
    S = x.shape[0]
    tm = INPROJ_TM
    a_shapes, a_specs = [], []
    for _, d in DIL_GROUPS:
        a_shapes.append(jax.ShapeDtypeStruct((3, d, S // d, DIL_OUT_W), BF16))
        a_specs.append(pl.BlockSpec((3, d, tm // d, DIL_OUT_W), lambda i: (0, 0, i, 0)))
    row = lambda width: pl.BlockSpec((tm, width), lambda i: (i, 0))
    return pl.pallas_call(
        _inproj_kernel,
        grid=(S // tm,),
        in_specs=[row(D_MODEL), pl.BlockSpec((1, tm), lambda i: (0, i)),
                  _resident(freqs_col), _resident(g), _resident(w)],
        out_specs=[row(RET_W)] + a_specs + [row(GATE_W)],
        out_shape=[jax.ShapeDtypeStruct((S, RET_W), BF16)] + a_shapes
                  + [jax.ShapeDtypeStruct((S, GATE_W), BF16)],
        scratch_shapes=[pltpu.VMEM((tm, D_MODEL), BF16), pltpu.VMEM((3 * DIL_W // LANES, tm, LANES), F32)],
        compiler_params=pltpu.CompilerParams(
            dimension_semantics=("parallel",), vmem_limit_bytes=56 * MIB),
        name="inproj",
    )(x, pos_row, freqs_col, g, w)


def _ret_kernel(pos_ref, invf_ref, dec_ref, qdec_ref, kdec_ref, q_ref, k_ref, v_ref, g_ref,
                o_ref, r_ref, *, chunk_dec):
    @pl.when(pl.program_id(0) == 0)
    def _():
        r_ref[...] = jnp.zeros_like(r_ref)

    half = RET_QK_DIM // 2
    ang = pos_ref[...] * invf_ref[...]
    cos = jnp.cos(ang)
    sin = jnp.sin(ang)
    for h in range(RET_HEADS):
        qo = h * RET_QK_DIM
        q1 = q_ref[:, qo:qo + half].astype(F32)
        q2 = q_ref[:, qo + half:qo + RET_QK_DIM].astype(F32)
        k1 = k_ref[:, qo:qo + half].astype(F32)
        k2 = k_ref[:, qo + half:qo + RET_QK_DIM].astype(F32)
        qr1 = q1 * cos - q2 * sin
        qr2 = q2 * cos + q1 * sin
        kr1 = k1 * cos - k2 * sin
        kr2 = k2 * cos + k1 * sin
        qdec = qdec_ref[h]
        kdec = kdec_ref[h]
        scale = RET_QK_DIM ** -0.5
        qr = jnp.concatenate([qr1, qr2], axis=-1).astype(BF16)
        qd = jnp.concatenate([qr1 * qdec, qr2 * qdec], axis=-1).astype(BF16)
        kr = jnp.concatenate([kr1 * scale, kr2 * scale], axis=-1).astype(BF16)
        kd = jnp.concatenate([kr1 * (kdec * scale), kr2 * (kdec * scale)], axis=-1).astype(BF16)
        vo = h * RET_V_DIM
        v = v_ref[:, vo:vo + RET_V_DIM]
        s = lax.dot_general(qr, kr, (((1,), (1,)), ((), ())), preferred_element_type=F32)
        s = s * dec_ref[h]
        inner = _bdot(s.astype(BF16), v)
        r = r_ref[h]
        cross = _bdot(qd, r.astype(BF16))
        upd = lax.dot_general(kd, v, (((0,), (0,)), ((), ())), preferred_element_type=F32)
        r_ref[h] = r * chunk_dec[h] + upd
        y = _rms(inner + cross)
        g = g_ref[:, vo:vo + RET_V_DIM].astype(F32)
        o_ref[:, vo:vo + RET_V_DIM] = (y * (g * jax.nn.sigmoid(g))).astype(BF16)


def _retention(ret, pos_col):
    S = ret.shape[0]
    C = RET_CHUNK
    half = RET_QK_DIM // 2
    H = RET_HEADS
    log_gamma = np.log1p(-(2.0 ** (-5.0 - np.arange(H, dtype=np.float64))))
    idx = np.arange(C, dtype=np.float64)
    diff = idx[:, None] - idx[None, :]
    dec = np.where(diff[None] >= 0, np.exp(np.maximum(diff, 0.0)[None] * log_gamma[:, None, None]), 0.0)
    qdec = np.exp((idx + 1.0)[None, :] * log_gamma[:, None])
    kdec = np.exp((C - 1.0 - idx)[None, :] * log_gamma[:, None])
    chunk_dec = tuple(float(v) for v in np.exp(C * log_gamma))
    qdec_b = jnp.asarray(np.broadcast_to(qdec[:, :, None], (H, C, half)), F32)
    kdec_b = jnp.asarray(np.broadcast_to(kdec[:, :, None], (H, C, half)), F32)
    dec = jnp.asarray(dec, F32)
    inv_freq = (1.0 / (RET_ROPE_BASE ** jnp.linspace(0.0, 1.0, half, dtype=F32))).reshape(1, half)
    return pl.pallas_call(
        functools.partial(_ret_kernel, chunk_dec=chunk_dec),
        grid=(S // C,),
        in_specs=[
            pl.BlockSpec((C, 1), lambda n: (n, 0)),
            _resident(inv_freq), _resident(dec), _resident(qdec_b), _resident(kdec_b),
            pl.BlockSpec((C, RET_QK_W), lambda n: (n, 0)),
            pl.BlockSpec((C, RET_QK_W), lambda n: (n, 1)),
            pl.BlockSpec((C, RET_V_W), lambda n: (n, (2 * RET_QK_W) // RET_V_W)),
            pl.BlockSpec((C, RET_V_W), lambda n: (n, (2 * RET_QK_W) // RET_V_W + 1)),
        ],
        out_specs=pl.BlockSpec((C, RET_V_W), lambda n: (n, 0)),
        out_shape=jax.ShapeDtypeStruct((S, RET_V_W), BF16),
        scratch_shapes=[pltpu.VMEM((H, RET_QK_DIM, RET_V_DIM), F32)],
        compiler_params=pltpu.CompilerParams(
            dimension_semantics=("arbitrary",), vmem_limit_bytes=40 * MIB),
        name="retention",
    )(pos_col, inv_freq, dec, qdec_b, kdec_b, ret, ret, ret, ret)


def _dil_kernel(band_ref, q_ref, k_ref, v_ref, o_ref, lse_ref, kprev_ref, vprev_ref):
    n = pl.program_id(1)
    QB = DIL_BLOCK

    @pl.when(n == 0)
    def _():
        kprev_ref[...] = jnp.zeros_like(kprev_ref)
        vprev_ref[...] = jnp.zeros_like(vprev_ref)

    col = lax.broadcasted_iota(jnp.int32, (QB, 2 * QB), 1)
    bias = jnp.where(jnp.logical_or(col >= QB, n > 0), band_ref[...], NEG_INF)
    lane = lax.broadcasted_iota(jnp.int32, (QB, LANES), 1)
    lse_tile = jnp.zeros((QB, LANES), F32)
    for pr in range(DIL_OUT_W // LANES):
        sl = slice(pr * LANES, (pr + 1) * LANES)
        q = q_ref[:, sl]
        kp = jnp.concatenate([kprev_ref[:, sl], k_ref[:, sl]], axis=0)
        vp = jnp.concatenate([vprev_ref[:, sl], v_ref[:, sl]], axis=0)
        outs = []
        for hh in range(2):
            in_head = (lane >= hh * DIL_HEAD_DIM) & (lane < (hh + 1) * DIL_HEAD_DIM)
            qm = jnp.where(in_head, q, jnp.zeros_like(q))
            s = lax.dot_general(qm, kp, (((1,), (1,)), ((), ())), preferred_element_type=F32) + bias
            m = jnp.max(s, axis=-1, keepdims=True)
            p = jnp.exp(s - m)
            l = jnp.sum(p, axis=-1, keepdims=True)
            outs.append(_bdot(p.astype(BF16), vp) / l)
            lse = m + jnp.log(l)
            head = 2 * pr + hh
            lse_tile = jnp.where((lane >= head * LSE_LANES) & (lane < (head + 1) * LSE_LANES), lse, lse_tile)
        o_ref[:, sl] = jnp.where(lane < DIL_HEAD_DIM, outs[0], outs[1]).astype(BF16)
    kprev_ref[...] = k_ref[...]
    vprev_ref[...] = v_ref[...]
    lse_ref[...] = lse_tile


def _dil_group(a, band_bias, g, d):
    L = a.shape[2]
    QB = DIL_BLOCK
    W = DIL_OUT_W
    sec = lambda which: pl.BlockSpec((None, None, QB, W), lambda c, n, which=which: (which, c, n, 0))
    return pl.pallas_call(
        _dil_kernel,
        grid=(d, L // QB),
        in_specs=[_resident(band_bias), sec(0), sec(1), sec(2)],
        out_specs=[pl.BlockSpec((None, QB, W), lambda c, n: (c, n, 0)),
                   pl.BlockSpec((None, QB, LANES), lambda c, n: (c, n, 0))],
        out_shape=[jax.ShapeDtypeStruct((d, L, W), BF16),
                   jax.ShapeDtypeStruct((d, L, LANES), F32)],
        scratch_shapes=[pltpu.VMEM((QB, W), BF16), pltpu.VMEM((QB, W), BF16)],
        compiler_params=pltpu.CompilerParams(
            dimension_semantics=("arbitrary", "arbitrary"), vmem_limit_bytes=32 * MIB),
        name=f"dil_attn_{g}",
    )(band_bias, a, a, a)


def _post_kernel(x_ref, yr_ref, o1_ref, o2_ref, o3_ref, l1_ref, l2_ref, l3_ref,
                 gr_ref, ga_ref, p_ref, e_ref,
                 bg_ref, wro_ref, wdo_ref, wo_ref, gpm_ref, gprm_ref, wup_ref, wdn_ref, gpom_ref,
                 gprp_ref, wpg_ref, bpg_ref, wpi_ref, gpop_ref, out_ref, osc_ref, lsc_ref):
    tm = x_ref.shape[0]

    def interleaved(ref, scr, slot):
        d, _, width = ref.shape
        if d == 1:
            return ref[0].astype(F32)
        ncol = width // LANES
        for c in range(d):
            v = ref[c].astype(F32)
            for cc in range(ncol):
                scr[slot * ncol + cc, pl.ds(c, tm // d, stride=d), :] = v[:, cc * LANES:(cc + 1) * LANES]
        return jnp.concatenate([scr[slot * ncol + cc] for cc in range(ncol)], axis=-1)

    o_refs, l_refs = (o1_ref, o2_ref, o3_ref), (l1_ref, l2_ref, l3_ref)
    os_ = [interleaved(r, osc_ref, i) for i, r in enumerate(o_refs)]
    l1, l2, l3 = [interleaved(r, lsc_ref, i) for i, r in enumerate(l_refs)]

    mx = jnp.maximum(jnp.maximum(l1, l2), l3)
    e1, e2, e3 = jnp.exp(l1 - mx), jnp.exp(l2 - mx), jnp.exp(l3 - mx)
    den = e1 + e2 + e3
    expand = e_ref[...]

    def widen(w):
        hi = w.astype(BF16)
        lo = (w - hi.astype(F32)).astype(BF16)
        return _bdot(hi, expand) + _bdot(lo, expand)

    merged = widen(e1 / den) * os_[0] + widen(e2 / den) * os_[1] + widen(e3 / den) * os_[2]
    yb = _bdot(merged.astype(BF16), wdo_ref[...])
    ya = _bdot(yr_ref[...], wro_ref[...])
    gr = gr_ref[...].astype(F32) + bg_ref[0:1, :]
    ga = ga_ref[...].astype(F32) + bg_ref[1:2, :]
    mixed = jax.nn.sigmoid(gr) * ya + jax.nn.sigmoid(ga) * yb
    h = x_ref[...] + _rms(_bdot(mixed.astype(BF16), wo_ref[...])) * gpm_ref[...]
    v2 = (_rms(h) * gprm_ref[...]).astype(BF16)
    f = jnp.zeros(h.shape, F32)
    for c in range(D_FF // FF_CHUNK):
        a = jnp.maximum(_bdot(v2, wup_ref[:, c * FF_CHUNK:(c + 1) * FF_CHUNK]), 0.0)
        f = f + _bdot((a * a).astype(BF16), wdn_ref[c * FF_CHUNK:(c + 1) * FF_CHUNK, :])
    h = h + _rms(f) * gpom_ref[...]
    gate = jax.nn.sigmoid(_bdot((_rms(h) * gprp_ref[...]).astype(BF16), wpg_ref[...]) + bpg_ref[...])
    e = _bdot(p_ref[...].astype(BF16), wpi_ref[...])
    out_ref[...] = h + _rms(gate * e) * gpop_ref[...]


def _post(x, yr, os_, lses, gates, p, expand, weights):
    S = x.shape[0]
    tm = POST_TM
    row = lambda w, j=0: pl.BlockSpec((tm, w), lambda i, j=j: (i, j))
    stream = lambda a: pl.BlockSpec((a.shape[0], tm // a.shape[0], a.shape[2]), lambda i: (0, i, 0))
    n_groups = len(DIL_GROUPS)
    return pl.pallas_call(
        _post_kernel,
        grid=(S // tm,),
        in_specs=[row(D_MODEL), row(RET_V_W)] + [stream(a) for a in os_] + [stream(a) for a in lses]
                 + [row(D_MODEL, 0), row(D_MODEL, 1), row(PLE_DIM), _resident(expand)]
                 + [_resident(w) for w in weights],
        out_specs=row(D_MODEL),
        out_shape=jax.ShapeDtypeStruct((S, D_MODEL), F32),
        scratch_shapes=[pltpu.VMEM((n_groups * DIL_OUT_W // LANES, tm, LANES), F32),
                        pltpu.VMEM((n_groups, tm, LANES), F32)],
        compiler_params=pltpu.CompilerParams(
            dimension_semantics=("parallel",), vmem_limit_bytes=56 * MIB),
        name="post",
    )(x, yr, *os_, *lses, gates, gates, p, expand, *weights)


def _dil_constants():
    QB = DIL_BLOCK
    qi = np.arange(QB)[:, None]
    kj = np.arange(2 * QB)[None, :] - QB
    dist = qi - kj
    band = np.where((dist >= 0) & (dist <= QB), 0.0, NEG_INF).astype(np.float32)
    expand = np.zeros((LANES, DIL_OUT_W), np.float32)
    for h in range(DIL_SLOTS):
        expand[h * LSE_LANES, h * DIL_HEAD_DIM:(h + 1) * DIL_HEAD_DIM] = 1.0
    return jnp.asarray(band), jnp.asarray(expand, BF16)


def kernel(x, p, positions, w_in, b_gate, w_ret_out, w_dil_out, w_o, g_pre_mix, g_post_mix, g_pre_mlp,
           g_post_mlp, w_up, w_down, g_pre_ple, w_ple_gate, b_ple_gate, w_ple_in, g_post_ple):
    B, S, _ = x.shape
    assert B == 1 and w_in.shape[0] == 1
    for window, dilation in DIL_GROUPS:
        assert window // dilation == DIL_BLOCK and S % (dilation * DIL_BLOCK) == 0
    xs = x[0]
    pos = positions[0].astype(F32)
    bf = lambda a: a[0].astype(BF16)
    vec = lambda a: a[0].reshape(1, -1)

    freqs = ROPE_THETA ** (-jnp.arange(0, ROPE_DIM, 2, dtype=F32) / ROPE_DIM)
    ret, a0, a1, a2, gates = _inproj(xs, pos.reshape(1, S), freqs.reshape(-1, 1), vec(g_pre_mix), bf(w_in))
    yr = _retention(ret, pos.reshape(S, 1))

    band_bias, expand = _dil_constants()
    os_, lses = [], []
    for g, ((_, d), a) in enumerate(zip(DIL_GROUPS, (a0, a1, a2))):
        o, lse = _dil_group(a, band_bias, g, d)
        os_.append(o)
        lses.append(lse)

    weights = (b_gate[0], bf(w_ret_out), bf(w_dil_out), bf(w_o), vec(g_post_mix), vec(g_pre_mlp), bf(w_up),
               bf(w_down), vec(g_post_mlp), vec(g_pre_ple), bf(w_ple_gate), vec(b_ple_gate), bf(w_ple_in),
               vec(g_post_ple))
    out = _post(xs, yr, os_, lses, gates, p[0, 0], expand, weights)
    return out[None]
```

```python
import functools

import numpy as np
import jax
import jax.numpy as jnp
from jax import lax
from jax.experimental import pallas as pl
from jax.experimental.pallas import tpu as pltpu

F32 = jnp.float32
BF16 = jnp.bfloat16

D_MODEL = 1024
PLE_DIM = 256
EPS = 1e-6
RET_HEADS = 4
RET_QK_DIM = 256
RET_V_DIM = 512
RET_ROPE_BASE = 10000.0
DIL_GROUPS = ((128, 1), (512, 4), (2048, 16))
DIL_SLOTS = 8
DIL_HEAD_DIM = 64
DIL_BLOCK = 128
ROPE_THETA = 500000.0
ROPE_DIM = DIL_HEAD_DIM // 4
D_FF = 4 * D_MODEL
RET_QK_W = RET_HEADS * RET_QK_DIM
RET_V_W = RET_HEADS * RET_V_DIM
RET_W = 2 * RET_QK_W + 2 * RET_V_W
DIL_OUT_W = DIL_SLOTS * DIL_HEAD_DIM
DIL_W = len(DIL_GROUPS) * DIL_OUT_W
GATE_W = 2 * D_MODEL
IN_W = RET_W + DIL_W * 3 + GATE_W
OFF_DIL = RET_W
OFF_GATE = RET_W + 3 * DIL_W

LANES = 128
NEG_INF = -1e30
LOG2_E = 1.4426950408889634
MIB = 1024 * 1024

INPROJ_TM = 256
INPROJ_TN = 512
RET_CHUNK = 256
POST_TM = 256
DIL_NB = 8
FF_CHUNK = 1024
LSE_LANES = LANES // DIL_SLOTS


def _rms(x):
    return x * lax.rsqrt(jnp.mean(x * x, axis=-1, keepdims=True) + EPS)


def _bdot(a, b):
    return jnp.dot(a, b, preferred_element_type=F32)


def _resident(a):
    return pl.BlockSpec(a.shape, lambda *_: (0,) * a.ndim, pipeline_mode=pl.Buffered(1))


def _inproj_kernel(x_ref, pos_ref, posc_ref, f_ref, invf_ref, qdec_ref, kdec_ref, g_ref, w_ref,
                   ret_ref, a0_ref, a1_ref, a2_ref, gate_ref, u_ref, scr_ref):
    tm = x_ref.shape[0]
    tn = INPROJ_TN
    u_ref[...] = (_rms(x_ref[...]) * g_ref[...]).astype(BF16)

    def proj(col):
        return _bdot(u_ref[...], w_ref[:, col:col + tn])

    for j in range(RET_V_W // tn):
        col = 2 * RET_QK_W + j * tn
        ret_ref[:, col:col + tn] = proj(col).astype(BF16)
    for j in range(RET_V_W // tn):
        col = 2 * RET_QK_W + RET_V_W + j * tn
        y = proj(col)
        ret_ref[:, col:col + tn] = (y * jax.nn.sigmoid(y)).astype(BF16)
    for j in range(GATE_W // tn):
        gate_ref[:, j * tn:(j + 1) * tn] = proj(OFF_GATE + j * tn).astype(BF16)

    ang_r = posc_ref[...] * invf_ref[...]
    cos_r, sin_r = jnp.cos(ang_r), jnp.sin(ang_r)
    half = RET_QK_DIM // 2

    ang = f_ref[...] * pos_ref[...]
    c8, s8 = jnp.cos(ang), jnp.sin(ang)
    hr = ROPE_DIM // 2
    rest = DIL_HEAD_DIM - ROPE_DIM
    reps = LANES // DIL_HEAD_DIM
    one, zr, z8 = jnp.ones((rest, tm), F32), jnp.zeros((rest, tm), F32), jnp.zeros((hr, tm), F32)
    cos_t = jnp.concatenate([c8, c8, one] * reps, axis=0).T
    sinp_t = jnp.concatenate([z8, s8, zr] * reps, axis=0).T
    sinm_t = jnp.concatenate([-s8, z8, zr] * reps, axis=0).T

    def rope(y, scale):
        cols = []
        for c in range(y.shape[1] // LANES):
            v = y[:, c * LANES:(c + 1) * LANES]
            r = v * cos_t + pltpu.roll(v, hr, 1) * sinp_t + pltpu.roll(v, LANES - hr, 1) * sinm_t
            cols.append(r * scale if scale != 1.0 else r)
        return jnp.concatenate(cols, axis=-1)

    heads_per_chunk = tn // RET_QK_DIM
    for sec, dec_ref in ((0, qdec_ref), (1, kdec_ref)):
        for j in range(RET_QK_W // tn):
            col = sec * RET_QK_W + j * tn
            y = proj(col)
            outs = []
            for hh in range(heads_per_chunk):
                y1 = y[:, hh * RET_QK_DIM:hh * RET_QK_DIM + half]
                y2 = y[:, hh * RET_QK_DIM + half:(hh + 1) * RET_QK_DIM]
                dec = dec_ref[j * heads_per_chunk + hh]
                outs += [(y1 * cos_r - y2 * sin_r) * dec, (y2 * cos_r + y1 * sin_r) * dec]
            ret_ref[:, col:col + tn] = jnp.concatenate(outs, axis=-1).astype(BF16)
    a_refs = (a0_ref, a1_ref, a2_ref)
    for which in range(3):
        for g, (_, d) in enumerate(DIL_GROUPS):
            y = proj(OFF_DIL + which * DIL_W + g * DIL_OUT_W)
            if which == 0:
                y = rope(y, LOG2_E * DIL_HEAD_DIM ** -0.5)
            elif which == 1:
                y = rope(y, 1.0)
            if d == 1:
                a_refs[g][which, 0] = y.astype(BF16)
            else:
                ncol = DIL_OUT_W // LANES
                s0 = (which * len(DIL_GROUPS) + g) * ncol
                for cc in range(ncol):
                    scr_ref[s0 + cc] = y[:, cc * LANES:(cc + 1) * LANES]
                for c in range(d):
                    a_refs[g][which, c] = jnp.concatenate(
                        [scr_ref[s0 + cc, pl.ds(c, tm // d, stride=d), :] for cc in range(ncol)],
                        axis=-1).astype(BF16)


def _ret_decay():
    return np.log1p(-(2.0 ** (-5.0 - np.arange(RET_HEADS, dtype=np.float64))))


def _inproj(x, pos_row, pos_col, freqs_col, g, w):
    S = x.shape[0]
    tm = INPROJ_TM
    C = RET_CHUNK
    assert tm % C == 0
    half = RET_QK_DIM // 2
    idx = (np.arange(tm) % C).astype(np.float64)
    log_gamma = _ret_decay()
    qdec = np.exp((idx + 1.0)[None, :] * log_gamma[:, None])
    kdec = np.exp(-(idx + 1.0)[None, :] * log_gamma[:, None]) * RET_QK_DIM ** -0.5
    widen = lambda t: jnp.asarray(np.broadcast_to(t[:, :, None], (RET_HEADS, tm, half)), F32)
    qdec, kdec = widen(qdec), widen(kdec)
    inv_freq = (1.0 / (RET_ROPE_BASE ** jnp.linspace(0.0, 1.0, half, dtype=F32))).reshape(1, half)
    a_shapes, a_specs = [], []
    for _, d in DIL_GROUPS:
        a_shapes.append(jax.ShapeDtypeStruct((3, d, S // d, DIL_OUT_W), BF16))
        a_specs.append(pl.BlockSpec((3, d, tm // d, DIL_OUT_W), lambda i: (0, 0, i, 0)))
    row = lambda width: pl.BlockSpec((tm, width), lambda i: (i, 0))
    return pl.pallas_call(
        _inproj_kernel,
        grid=(S // tm,),
        in_specs=[row(D_MODEL), pl.BlockSpec((1, tm), lambda i: (0, i)), row(1),
                  _resident(freqs_col), _resident(inv_freq), _resident(qdec), _resident(kdec),
                  _resident(g), _resident(w)],
        out_specs=[row(RET_W)] + a_specs + [row(GATE_W)],
        out_shape=[jax.ShapeDtypeStruct((S, RET_W), BF16)] + a_shapes
                  + [jax.ShapeDtypeStruct((S, GATE_W), BF16)],
        scratch_shapes=[pltpu.VMEM((tm, D_MODEL), BF16), pltpu.VMEM((3 * DIL_W // LANES, tm, LANES), F32)],
        compiler_params=pltpu.CompilerParams(
            dimension_semantics=("parallel",), vmem_limit_bytes=56 * MIB),
        name="inproj",
    )(x, pos_row, pos_col, freqs_col, inv_freq, qdec, kdec, g, w)


def _ret_kernel(q_ref, k_ref, v_ref, g_ref, o_ref, r_ref, *, chunk_dec):
    @pl.when(pl.program_id(0) == 0)
    def _():
        r_ref[...] = jnp.zeros_like(r_ref)

    C = q_ref.shape[0]
    causal = lax.broadcasted_iota(jnp.int32, (C, C), 0) >= lax.broadcasted_iota(jnp.int32, (C, C), 1)
    for h in range(RET_HEADS):
        q = q_ref[:, h * RET_QK_DIM:(h + 1) * RET_QK_DIM]
        k = k_ref[:, h * RET_QK_DIM:(h + 1) * RET_QK_DIM]
        vo = h * RET_V_DIM
        v = v_ref[:, vo:vo + RET_V_DIM]
        s = lax.dot_general(q, k, (((1,), (1,)), ((), ())), preferred_element_type=F32)
        s = jnp.where(causal, s, 0.0).astype(BF16)
        r = r_ref[h]
        y = _bdot(s, v) + _bdot(q, r.astype(BF16))
        upd = lax.dot_general(k, v, (((0,), (0,)), ((), ())), preferred_element_type=F32)
        r_ref[h] = (r + upd) * chunk_dec[h]
        o_ref[:, vo:vo + RET_V_DIM] = (_rms(y) * g_ref[:, vo:vo + RET_V_DIM].astype(F32)).astype(BF16)


def _retention(ret):
    S = ret.shape[0]
    C = RET_CHUNK
    H = RET_HEADS
    chunk_dec = tuple(float(v) for v in np.exp(C * _ret_decay()))
    return pl.pallas_call(
        functools.partial(_ret_kernel, chunk_dec=chunk_dec),
        grid=(S // C,),
        in_specs=[
            pl.BlockSpec((C, RET_QK_W), lambda n: (n, 0)),
            pl.BlockSpec((C, RET_QK_W), lambda n: (n, 1)),
            pl.BlockSpec((C, RET_V_W), lambda n: (n, (2 * RET_QK_W) // RET_V_W)),
            pl.BlockSpec((C, RET_V_W), lambda n: (n, (2 * RET_QK_W) // RET_V_W + 1)),
        ],
        out_specs=pl.BlockSpec((C, RET_V_W), lambda n: (n, 0)),
        out_shape=jax.ShapeDtypeStruct((S, RET_V_W), BF16),
        scratch_shapes=[pltpu.VMEM((H, RET_QK_DIM, RET_V_DIM), F32)],
        compiler_params=pltpu.CompilerParams(
            dimension_semantics=("arbitrary",), vmem_limit_bytes=40 * MIB),
        name="retention",
    )(ret, ret, ret, ret)


def _dil_kernel(band_ref, q_ref, k_ref, v_ref, kprev_ref, vprev_ref, o_ref, m_ref, l_ref):
    n = pl.program_id(1)
    QB = DIL_BLOCK
    lane = lax.broadcasted_iota(jnp.int32, (QB, LANES), 1)
    for b in range(q_ref.shape[0] // QB):
        rows = slice(b * QB, (b + 1) * QB)
        bias = band_ref[jnp.minimum(n, 1)] if b == 0 else band_ref[1]
        m_tile = jnp.zeros((QB, LANES), F32)
        l_tile = jnp.zeros((QB, LANES), F32)
        for pr in range(DIL_OUT_W // LANES):
            sl = slice(pr * LANES, (pr + 1) * LANES)
            q = q_ref[rows, sl]
            if b == 0:
                kp = jnp.concatenate([kprev_ref[:, sl], k_ref[rows, sl]], axis=0)
                vp = jnp.concatenate([vprev_ref[:, sl], v_ref[rows, sl]], axis=0)
            else:
                kp = k_ref[(b - 1) * QB:(b + 1) * QB, sl]
                vp = v_ref[(b - 1) * QB:(b + 1) * QB, sl]
            zero = jnp.zeros_like(q)
            qs = jnp.concatenate([jnp.where(lane < DIL_HEAD_DIM, q, zero),
                                  jnp.where(lane >= DIL_HEAD_DIM, q, zero)], axis=0)
            s = lax.dot_general(qs, kp, (((1,), (1,)), ((), ())), preferred_element_type=F32)
            ps = []
            for hh in range(2):
                sh = s[hh * QB:(hh + 1) * QB] + bias
                m = jnp.max(sh, axis=-1, keepdims=True)
                p = jnp.exp2(sh - m)
                l = jnp.sum(p, axis=-1, keepdims=True)
                ps.append(p.astype(BF16))
                head = 2 * pr + hh
                in_slot = (lane >= head * LSE_LANES) & (lane < (head + 1) * LSE_LANES)
                m_tile = jnp.where(in_slot, m, m_tile)
                l_tile = jnp.where(in_slot, l, l_tile)
            o2 = _bdot(jnp.concatenate(ps, axis=0), vp)
            o_ref[rows, sl] = jnp.where(lane < DIL_HEAD_DIM, o2[:QB], o2[QB:]).astype(BF16)
        m_ref[rows, :] = m_tile
        l_ref[rows, :] = l_tile


def _dil_group(a, band_bias, g, d):
    L = a.shape[2]
    QB = DIL_BLOCK
    W = DIL_OUT_W
    nb = DIL_NB
    T = nb * QB
    sec = lambda which: pl.BlockSpec((None, None, T, W), lambda c, n, which=which: (which, c, n, 0))
    halo = lambda which: pl.BlockSpec(
        (None, None, QB, W), lambda c, n, which=which: (which, c, jnp.maximum(n * nb - 1, 0), 0))
    return pl.pallas_call(
        _dil_kernel,
        grid=(d, L // T),
        in_specs=[_resident(band_bias), sec(0), sec(1), sec(2), halo(1), halo(2)],
        out_specs=[pl.BlockSpec((None, T, W), lambda c, n: (c, n, 0)),
                   pl.BlockSpec((None, T, LANES), lambda c, n: (c, n, 0)),
                   pl.BlockSpec((None, T, LANES), lambda c, n: (c, n, 0))],
        out_shape=[jax.ShapeDtypeStruct((d, L, W), BF16),
                   jax.ShapeDtypeStruct((d, L, LANES), F32),
                   jax.ShapeDtypeStruct((d, L, LANES), F32)],
        compiler_params=pltpu.CompilerParams(
            dimension_semantics=("parallel", "parallel"), vmem_limit_bytes=32 * MIB),
        name=f"dil_attn_{g}",
    )(band_bias, a, a, a, a, a)


def _post_kernel(x_ref, yr_ref, o1_ref, o2_ref, o3_ref, m1_ref, m2_ref, m3_ref, l1_ref, l2_ref, l3_ref,
                 gr_ref, ga_ref, p_ref, e_ref,
                 bg_ref, wro_ref, wdo_ref, wo_ref, gpm_ref, gprm_ref, wup_ref, wdn_ref, gpom_ref,
                 gprp_ref, wpg_ref, bpg_ref, wpi_ref, gpop_ref, out_ref, osc_ref, lsc_ref):
    tm = x_ref.shape[0]

    def interleaved(ref, scr, slot):
        d, _, width = ref.shape
        if d == 1:
            return ref[0].astype(F32)
        ncol = width // LANES
        for c in range(d):
            v = ref[c].astype(F32)
            for cc in range(ncol):
                scr[slot * ncol + cc, pl.ds(c, tm // d, stride=d), :] = v[:, cc * LANES:(cc + 1) * LANES]
        return jnp.concatenate([scr[slot * ncol + cc] for cc in range(ncol)], axis=-1)

    n_groups = len(DIL_GROUPS)
    os_ = [interleaved(r, osc_ref, i) for i, r in enumerate((o1_ref, o2_ref, o3_ref))]
    m1, m2, m3 = [interleaved(r, lsc_ref, i) for i, r in enumerate((m1_ref, m2_ref, m3_ref))]
    l1, l2, l3 = [interleaved(r, lsc_ref, n_groups + i) for i, r in enumerate((l1_ref, l2_ref, l3_ref))]

    mx = jnp.maximum(jnp.maximum(m1, m2), m3)
    e1, e2, e3 = jnp.exp2(m1 - mx), jnp.exp2(m2 - mx), jnp.exp2(m3 - mx)
    den = e1 * l1 + e2 * l2 + e3 * l3
    expand = e_ref[...]

    def widen(w):
        hi = w.astype(BF16)
        lo = (w - hi.astype(F32)).astype(BF16)
        return _bdot(hi, expand) + _bdot(lo, expand)

    merged = widen(e1 / den) * os_[0] + widen(e2 / den) * os_[1] + widen(e3 / den) * os_[2]
    yb = _bdot(merged.astype(BF16), wdo_ref[...])
    ya = _bdot(yr_ref[...], wro_ref[...])
    gr = gr_ref[...].astype(F32) + bg_ref[0:1, :]
    ga = ga_ref[...].astype(F32) + bg_ref[1:2, :]
    mixed = jax.nn.sigmoid(gr) * ya + jax.nn.sigmoid(ga) * yb
    h = x_ref[...] + _rms(_bdot(mixed.astype(BF16), wo_ref[...])) * gpm_ref[...]
    v2 = (_rms(h) * gprm_ref[...]).astype(BF16)
    f = jnp.zeros(h.shape, F32)
    for c in range(D_FF // FF_CHUNK):
        a = jnp.maximum(_bdot(v2, wup_ref[:, c * FF_CHUNK:(c + 1) * FF_CHUNK]), 0.0)
        f = f + _bdot((a * a).astype(BF16), wdn_ref[c * FF_CHUNK:(c + 1) * FF_CHUNK, :])
    h = h + _rms(f) * gpom_ref[...]
    gate = jax.nn.sigmoid(_bdot((_rms(h) * gprp_ref[...]).astype(BF16), wpg_ref[...]) + bpg_ref[...])
    e = _bdot(p_ref[...].astype(BF16), wpi_ref[...])
    out_ref[...] = h + _rms(gate * e) * gpop_ref[...]


def _post(x, yr, os_, ms, ls, gates, p, expand, weights):
    S = x.shape[0]
    tm = POST_TM
    row = lambda w, j=0: pl.BlockSpec((tm, w), lambda i, j=j: (i, j))
    stream = lambda a: pl.BlockSpec((a.shape[0], tm // a.shape[0], a.shape[2]), lambda i: (0, i, 0))
    n_groups = len(DIL_GROUPS)
    return pl.pallas_call(
        _post_kernel,
        grid=(S // tm,),
        in_specs=[row(D_MODEL), row(RET_V_W)] + [stream(a) for a in (*os_, *ms, *ls)]
                 + [row(D_MODEL, 0), row(D_MODEL, 1), row(PLE_DIM), _resident(expand)]
                 + [_resident(w) for w in weights],
        out_specs=row(D_MODEL),
        out_shape=jax.ShapeDtypeStruct((S, D_MODEL), F32),
        scratch_shapes=[pltpu.VMEM((n_groups * DIL_OUT_W // LANES, tm, LANES), F32),
                        pltpu.VMEM((2 * n_groups, tm, LANES), F32)],
        compiler_params=pltpu.CompilerParams(
            dimension_semantics=("parallel",), vmem_limit_bytes=56 * MIB),
        name="post",
    )(x, yr, *os_, *ms, *ls, gates, gates, p, expand, *weights)


def _dil_constants():
    QB = DIL_BLOCK
    qi = np.arange(QB)[:, None]
    kj = np.arange(2 * QB)[None, :] - QB
    dist = qi - kj
    in_band = (dist >= 0) & (dist <= QB)
    band = np.where(np.stack([in_band & (kj >= 0), in_band]), 0.0, NEG_INF).astype(np.float32)
    expand = np.zeros((LANES, DIL_OUT_W), np.float32)
    for h in range(DIL_SLOTS):
        expand[h * LSE_LANES, h * DIL_HEAD_DIM:(h + 1) * DIL_HEAD_DIM] = 1.0
    return jnp.asarray(band), jnp.asarray(expand, BF16)


def kernel(x, p, positions, w_in, b_gate, w_ret_out, w_dil_out, w_o, g_pre_mix, g_post_mix, g_pre_mlp,
           g_post_mlp, w_up, w_down, g_pre_ple, w_ple_gate, b_ple_gate, w_ple_in, g_post_ple):
    B, S, _ = x.shape
    assert B == 1 and w_in.shape[0] == 1
    for window, dilation in DIL_GROUPS:
        assert window // dilation == DIL_BLOCK and S % (dilation * DIL_BLOCK) == 0
    xs = x[0]
    pos = positions[0].astype(F32)
    bf = lambda a: a[0].astype(BF16)
    vec = lambda a: a[0].reshape(1, -1)

    freqs = ROPE_THETA ** (-jnp.arange(0, ROPE_DIM, 2, dtype=F32) / ROPE_DIM)
    ret, a0, a1, a2, gates = _inproj(xs, pos.reshape(1, S), pos.reshape(S, 1), freqs.reshape(-1, 1),
                                     vec(g_pre_mix), bf(w_in))
    yr = _retention(ret)

    band_bias, expand = _dil_constants()
    os_, ms, ls = [], [], []
    for g, ((_, d), a) in enumerate(zip(DIL_GROUPS, (a0, a1, a2))):
        o, m, l = _dil_group(a, band_bias, g, d)
        os_.append(o)
        ms.append(m)
        ls.append(l)

    weights = (b_gate[0], bf(w_ret_out), bf(w_dil_out), bf(w_o), vec(g_post_mix), vec(g_pre_mlp), bf(w_up),
               bf(w_down), vec(g_post_mlp), vec(g_pre_ple), bf(w_ple_gate), vec(b_ple_gate), bf(w_ple_in),
               vec(g_post_ple))
    out = _post(xs, yr, os_, ms, ls, gates, p[0, 0], expand, weights)
    return out[None]
```

```python
import functools

import numpy as np
import jax
import jax.numpy as jnp
from jax import lax
from jax.experimental import pallas as pl
from jax.experimental.pallas import tpu as pltpu

F32 = jnp.float32
BF16 = jnp.bfloat16

D_MODEL = 1024
PLE_DIM = 256
EPS = 1e-6
RET_HEADS = 4
RET_QK_DIM = 256
RET_V_DIM = 512
RET_ROPE_BASE = 10000.0
DIL_GROUPS = ((128, 1), (512, 4), (2048, 16))
DIL_SLOTS = 8
DIL_HEAD_DIM = 64
DIL_BLOCK = 128
ROPE_THETA = 500000.0
ROPE_DIM = DIL_HEAD_DIM // 4
D_FF = 4 * D_MODEL
RET_QK_W = RET_HEADS * RET_QK_DIM
RET_V_W = RET_HEADS * RET_V_DIM
RET_W = 2 * RET_QK_W + 2 * RET_V_W
DIL_OUT_W = DIL_SLOTS * DIL_HEAD_DIM
DIL_W = len(DIL_GROUPS) * DIL_OUT_W
GATE_W = 2 * D_MODEL
IN_W = RET_W + DIL_W * 3 + GATE_W
OFF_DIL = RET_W
OFF_GATE = RET_W + 3 * DIL_W

LANES = 128
NEG_INF = -1e30
LOG2_E = 1.4426950408889634
MIB = 1024 * 1024

INPROJ_TM = 256
INPROJ_TN = 512
POST_TM = 256
DIL_NB = 8
FF_CHUNK = 1024
LSE_LANES = LANES // DIL_SLOTS


def _rms(x):
    return x * lax.rsqrt(jnp.mean(x * x, axis=-1, keepdims=True) + EPS)


def _bdot(a, b):
    return jnp.dot(a, b, preferred_element_type=F32)


def _resident(a):
    return pl.BlockSpec(a.shape, lambda *_: (0,) * a.ndim, pipeline_mode=pl.Buffered(1))


def _inproj_kernel(x_ref, pos_ref, posc_ref, f_ref, invf_ref, qdec_ref, kdec_ref, g_ref, w_ref,
                   yr_ref, a0_ref, a1_ref, a2_ref, gate_ref,
                   u_ref, tab_ref, scr_ref, ret_ref, r_ref, *, chunk_dec):
    tm = x_ref.shape[0]
    tn = INPROJ_TN
    half = RET_QK_DIM // 2
    hr = ROPE_DIM // 2

    @pl.when(pl.program_id(0) == 0)
    def _():
        r_ref[...] = jnp.zeros_like(r_ref)

    u_ref[...] = (_rms(x_ref[...]) * g_ref[...]).astype(BF16)

    def proj(col):
        return _bdot(u_ref[...], w_ref[:, col:col + tn])

    for j in range(RET_V_W // tn):
        col = 2 * RET_QK_W + j * tn
        ret_ref[:, col:col + tn] = proj(col).astype(BF16)

    ang_r = posc_ref[...] * invf_ref[...]
    tab_ref[0] = jnp.cos(ang_r)
    tab_ref[1] = jnp.sin(ang_r)
    ang = f_ref[...] * pos_ref[...]
    c8, s8 = jnp.cos(ang), jnp.sin(ang)
    rest = DIL_HEAD_DIM - ROPE_DIM
    reps = LANES // DIL_HEAD_DIM
    one, zr, z8 = jnp.ones((rest, tm), F32), jnp.zeros((rest, tm), F32), jnp.zeros((hr, tm), F32)
    tab_ref[2] = jnp.concatenate([c8, c8, one] * reps, axis=0).T
    tab_ref[3] = jnp.concatenate([z8, s8, zr] * reps, axis=0).T
    tab_ref[4] = jnp.concatenate([-s8, z8, zr] * reps, axis=0).T

    for j in range(RET_V_W // tn):
        col = 2 * RET_QK_W + RET_V_W + j * tn
        y = proj(col)
        ret_ref[:, col:col + tn] = (y * jax.nn.sigmoid(y)).astype(BF16)
    for j in range(GATE_W // tn):
        gate_ref[:, j * tn:(j + 1) * tn] = proj(OFF_GATE + j * tn).astype(BF16)

    def rope(y, scale):
        cols = []
        for c in range(y.shape[1] // LANES):
            v = y[:, c * LANES:(c + 1) * LANES]
            r = v * tab_ref[2] + pltpu.roll(v, hr, 1) * tab_ref[3] + pltpu.roll(v, LANES - hr, 1) * tab_ref[4]
            cols.append(r * scale if scale != 1.0 else r)
        return jnp.concatenate(cols, axis=-1)

    heads_per_chunk = tn // RET_QK_DIM
    for sec, dec_ref in ((0, qdec_ref), (1, kdec_ref)):
        for j in range(RET_QK_W // tn):
            col = sec * RET_QK_W + j * tn
            y = proj(col)
            outs = []
            for hh in range(heads_per_chunk):
                y1 = y[:, hh * RET_QK_DIM:hh * RET_QK_DIM + half]
                y2 = y[:, hh * RET_QK_DIM + half:(hh + 1) * RET_QK_DIM]
                dec = dec_ref[j * heads_per_chunk + hh]
                cos_r, sin_r = tab_ref[0], tab_ref[1]
                outs += [(y1 * cos_r - y2 * sin_r) * dec, (y2 * cos_r + y1 * sin_r) * dec]
            ret_ref[:, col:col + tn] = jnp.concatenate(outs, axis=-1).astype(BF16)
    a_refs = (a0_ref, a1_ref, a2_ref)

    def dil_section(which, g):
        d = DIL_GROUPS[g][1]
        y = proj(OFF_DIL + which * DIL_W + g * DIL_OUT_W)
        if which == 0:
            y = rope(y, LOG2_E * DIL_HEAD_DIM ** -0.5)
        elif which == 1:
            y = rope(y, 1.0)
        if d == 1:
            a_refs[g][which, 0] = y.astype(BF16)
        else:
            ncol = DIL_OUT_W // LANES
            s0 = (which * len(DIL_GROUPS) + g) * ncol
            for cc in range(ncol):
                scr_ref[s0 + cc] = y[:, cc * LANES:(cc + 1) * LANES]
            for c in range(d):
                a_refs[g][which, c] = jnp.concatenate(
                    [scr_ref[s0 + cc, pl.ds(c, tm // d, stride=d), :] for cc in range(ncol)],
                    axis=-1).astype(BF16)

    causal = lax.broadcasted_iota(jnp.int32, (tm, tm), 0) >= lax.broadcasted_iota(jnp.int32, (tm, tm), 1)

    def retention_head(h):
        q = ret_ref[:, h * RET_QK_DIM:(h + 1) * RET_QK_DIM]
        k = ret_ref[:, RET_QK_W + h * RET_QK_DIM:RET_QK_W + (h + 1) * RET_QK_DIM]
        vo = 2 * RET_QK_W + h * RET_V_DIM
        go = vo + RET_V_W
        v = ret_ref[:, vo:vo + RET_V_DIM]
        s = lax.dot_general(q, k, (((1,), (1,)), ((), ())), preferred_element_type=F32)
        s = jnp.where(causal, s, 0.0).astype(BF16)
        r = r_ref[h]
        y = _bdot(s, v) + _bdot(q, r.astype(BF16))
        upd = lax.dot_general(k, v, (((0,), (0,)), ((), ())), preferred_element_type=F32)
        r_ref[h] = (r + upd) * chunk_dec[h]
        yr_ref[:, h * RET_V_DIM:(h + 1) * RET_V_DIM] = (
            _rms(y) * ret_ref[:, go:go + RET_V_DIM].astype(F32)).astype(BF16)

    sections = [(which, g) for which in range(3) for g in range(len(DIL_GROUPS))]
    per_head = -(-len(sections) // RET_HEADS)
    for h in range(RET_HEADS):
        retention_head(h)
        for which, g in sections[h * per_head:(h + 1) * per_head]:
            dil_section(which, g)


def _ret_decay():
    return np.log1p(-(2.0 ** (-5.0 - np.arange(RET_HEADS, dtype=np.float64))))


def _inproj(x, pos_row, pos_col, freqs_col, g, w):
    S = x.shape[0]
    tm = INPROJ_TM
    half = RET_QK_DIM // 2
    idx = np.arange(tm, dtype=np.float64)
    log_gamma = _ret_decay()
    chunk_dec = tuple(float(v) for v in np.exp(tm * log_gamma))
    qdec = np.exp((idx + 1.0)[None, :] * log_gamma[:, None])
    kdec = np.exp(-(idx + 1.0)[None, :] * log_gamma[:, None]) * RET_QK_DIM ** -0.5
    widen = lambda t: jnp.asarray(np.broadcast_to(t[:, :, None], (RET_HEADS, tm, half)), F32)
    qdec, kdec = widen(qdec), widen(kdec)
    inv_freq = (1.0 / (RET_ROPE_BASE ** jnp.linspace(0.0, 1.0, half, dtype=F32))).reshape(1, half)
    a_shapes, a_specs = [], []
    for _, d in DIL_GROUPS:
        a_shapes.append(jax.ShapeDtypeStruct((3, d, S // d, DIL_OUT_W), BF16))
        a_specs.append(pl.BlockSpec((3, d, tm // d, DIL_OUT_W), lambda i: (0, 0, i, 0)))
    nt = S // tm
    row = lambda width: pl.BlockSpec((tm, width), lambda i: (i, 0))
    return pl.pallas_call(
        functools.partial(_inproj_kernel, chunk_dec=chunk_dec),
        grid=(nt,),
        in_specs=[row(D_MODEL), pl.BlockSpec((1, tm), lambda i: (0, i)), row(1),
                  _resident(freqs_col), _resident(inv_freq), _resident(qdec), _resident(kdec),
                  _resident(g), _resident(w)],
        out_specs=[row(RET_V_W)] + a_specs + [row(GATE_W)],
        out_shape=[jax.ShapeDtypeStruct((S, RET_V_W), BF16)] + a_shapes
                  + [jax.ShapeDtypeStruct((S, GATE_W), BF16)],
        scratch_shapes=[pltpu.VMEM((tm, D_MODEL), BF16), pltpu.VMEM((5, tm, LANES), F32),
                        pltpu.VMEM((3 * DIL_W // LANES, tm, LANES), F32),
                        pltpu.VMEM((tm, RET_W), BF16), pltpu.VMEM((RET_HEADS, RET_QK_DIM, RET_V_DIM), F32)],
        compiler_params=pltpu.CompilerParams(
            dimension_semantics=("arbitrary",), vmem_limit_bytes=56 * MIB),
        name="inproj",
    )(x, pos_row, pos_col, freqs_col, inv_freq, qdec, kdec, g, w)


def _dil_kernel(band_ref, q_ref, k_ref, v_ref, kprev_ref, vprev_ref, o_ref, m_ref, l_ref):
    n = pl.program_id(1)
    QB = DIL_BLOCK
    lane = lax.broadcasted_iota(jnp.int32, (QB, LANES), 1)
    for b in range(q_ref.shape[0] // QB):
        rows = slice(b * QB, (b + 1) * QB)
        bias = band_ref[jnp.minimum(n, 1)] if b == 0 else band_ref[1]
        m_tile = jnp.zeros((QB, LANES), F32)
        l_tile = jnp.zeros((QB, LANES), F32)
        for pr in range(DIL_OUT_W // LANES):
            sl = slice(pr * LANES, (pr + 1) * LANES)
            q = q_ref[rows, sl]
            if b == 0:
                kp = jnp.concatenate([kprev_ref[:, sl], k_ref[rows, sl]], axis=0)
                vp = jnp.concatenate([vprev_ref[:, sl], v_ref[rows, sl]], axis=0)
            else:
                kp = k_ref[(b - 1) * QB:(b + 1) * QB, sl]
                vp = v_ref[(b - 1) * QB:(b + 1) * QB, sl]
            zero = jnp.zeros_like(q)
            qs = jnp.concatenate([jnp.where(lane < DIL_HEAD_DIM, q, zero),
                                  jnp.where(lane >= DIL_HEAD_DIM, q, zero)], axis=0)
            s = lax.dot_general(qs, kp, (((1,), (1,)), ((), ())), preferred_element_type=F32)
            ps = []
            for hh in range(2):
                sh = s[hh * QB:(hh + 1) * QB] + bias
                m = jnp.max(sh, axis=-1, keepdims=True)
                p = jnp.exp2(sh - m)
                l = jnp.sum(p, axis=-1, keepdims=True)
                ps.append(p.astype(BF16))
                head = 2 * pr + hh
                in_slot = (lane >= head * LSE_LANES) & (lane < (head + 1) * LSE_LANES)
                m_tile = jnp.where(in_slot, m, m_tile)
                l_tile = jnp.where(in_slot, l, l_tile)
            o2 = _bdot(jnp.concatenate(ps, axis=0), vp)
            o_ref[rows, sl] = jnp.where(lane < DIL_HEAD_DIM, o2[:QB], o2[QB:]).astype(BF16)
        m_ref[rows, :] = m_tile
        l_ref[rows, :] = l_tile


def _dil_group(a, band_bias, g, d):
    L = a.shape[2]
    QB = DIL_BLOCK
    W = DIL_OUT_W
    nb = DIL_NB
    T = nb * QB
    sec = lambda which: pl.BlockSpec((None, None, T, W), lambda c, n, which=which: (which, c, n, 0))
    halo = lambda which: pl.BlockSpec(
        (None, None, QB, W), lambda c, n, which=which: (which, c, jnp.maximum(n * nb - 1, 0), 0))
    return pl.pallas_call(
        _dil_kernel,
        grid=(d, L // T),
        in_specs=[_resident(band_bias), sec(0), sec(1), sec(2), halo(1), halo(2)],
        out_specs=[pl.BlockSpec((None, T, W), lambda c, n: (c, n, 0)),
                   pl.BlockSpec((None, T, LANES), lambda c, n: (c, n, 0)),
                   pl.BlockSpec((None, T, LANES), lambda c, n: (c, n, 0))],
        out_shape=[jax.ShapeDtypeStruct((d, L, W), BF16),
                   jax.ShapeDtypeStruct((d, L, LANES), F32),
                   jax.ShapeDtypeStruct((d, L, LANES), F32)],
        compiler_params=pltpu.CompilerParams(
            dimension_semantics=("parallel", "parallel"), vmem_limit_bytes=32 * MIB),
        name=f"dil_attn_{g}",
    )(band_bias, a, a, a, a, a)


def _post_kernel(x_ref, yr_ref, o1_ref, o2_ref, o3_ref, m1_ref, m2_ref, m3_ref, l1_ref, l2_ref, l3_ref,
                 gr_ref, ga_ref, p_ref, e_ref,
                 bg_ref, wro_ref, wdo_ref, wo_ref, gpm_ref, gprm_ref, wup_ref, wdn_ref, gpom_ref,
                 gprp_ref, wpg_ref, bpg_ref, wpi_ref, gpop_ref, out_ref, osc_ref, lsc_ref, h_ref, v2_ref):
    tm = x_ref.shape[0]

    @pl.when(pl.program_id(0) == 0)
    def _():
        h_ref[...] = jnp.zeros_like(h_ref)
        v2_ref[...] = jnp.zeros_like(v2_ref)

    def mlp_chunk(c, f):
        a = jnp.maximum(_bdot(v2_ref[...], wup_ref[:, c * FF_CHUNK:(c + 1) * FF_CHUNK]), 0.0)
        return f + _bdot((a * a).astype(BF16), wdn_ref[c * FF_CHUNK:(c + 1) * FF_CHUNK, :])

    f = mlp_chunk(0, jnp.zeros((tm, D_MODEL), F32))

    def interleaved(ref, scr, slot):
        d, _, width = ref.shape
        if d == 1:
            return ref[0].astype(F32)
        ncol = width // LANES
        for c in range(d):
            v = ref[c].astype(F32)
            for cc in range(ncol):
                scr[slot * ncol + cc, pl.ds(c, tm // d, stride=d), :] = v[:, cc * LANES:(cc + 1) * LANES]
        return jnp.concatenate([scr[slot * ncol + cc] for cc in range(ncol)], axis=-1)

    n_groups = len(DIL_GROUPS)
    os_ = [interleaved(r, osc_ref, i) for i, r in enumerate((o1_ref, o2_ref, o3_ref))]
    m1, m2, m3 = [interleaved(r, lsc_ref, i) for i, r in enumerate((m1_ref, m2_ref, m3_ref))]
    l1, l2, l3 = [interleaved(r, lsc_ref, n_groups + i) for i, r in enumerate((l1_ref, l2_ref, l3_ref))]

    mx = jnp.maximum(jnp.maximum(m1, m2), m3)
    e1, e2, e3 = jnp.exp2(m1 - mx), jnp.exp2(m2 - mx), jnp.exp2(m3 - mx)
    den = e1 * l1 + e2 * l2 + e3 * l3
    expand = e_ref[...]

    def widen(w):
        return _bdot(w.astype(BF16), expand)

    merged = (widen(e1 / den) * os_[0] + widen(e2 / den) * os_[1] + widen(e3 / den) * os_[2]).astype(BF16)
    f = mlp_chunk(1, f)
    sig_r = jax.nn.sigmoid(gr_ref[...].astype(F32) + bg_ref[0:1, :])
    sig_a = jax.nn.sigmoid(ga_ref[...].astype(F32) + bg_ref[1:2, :])
    f = mlp_chunk(2, f)
    mixed = (sig_r * _bdot(yr_ref[...], wro_ref[...]) + sig_a * _bdot(merged, wdo_ref[...])).astype(BF16)
    f = mlp_chunk(3, f)
    e = _bdot(p_ref[...].astype(BF16), wpi_ref[...])
    t = _bdot(mixed, wo_ref[...])
    h = h_ref[...] + _rms(f) * gpom_ref[...]
    hg = (_rms(h) * gprp_ref[...]).astype(BF16)
    gate_pre = _bdot(hg, wpg_ref[...])
    h1 = x_ref[...] + _rms(t) * gpm_ref[...]
    h_ref[...] = h1
    v2_ref[...] = (_rms(h1) * gprm_ref[...]).astype(BF16)
    out_ref[...] = h + _rms(jax.nn.sigmoid(gate_pre + bpg_ref[...]) * e) * gpop_ref[...]


def _post(x, yr, os_, ms, ls, gates, p, expand, weights):
    S = x.shape[0]
    tm = POST_TM
    nt = S // tm
    cur = lambda i: jnp.minimum(i, nt - 1)
    prev = lambda i: jnp.maximum(i - 1, 0)
    row = lambda w, j=0: pl.BlockSpec((tm, w), lambda i, j=j: (cur(i), j))
    stream = lambda a: pl.BlockSpec((a.shape[0], tm // a.shape[0], a.shape[2]), lambda i: (0, cur(i), 0))
    n_groups = len(DIL_GROUPS)
    return pl.pallas_call(
        _post_kernel,
        grid=(nt + 1,),
        in_specs=[row(D_MODEL), row(RET_V_W)] + [stream(a) for a in (*os_, *ms, *ls)]
                 + [row(D_MODEL, 0), row(D_MODEL, 1),
                    pl.BlockSpec((tm, PLE_DIM), lambda i: (prev(i), 0)), _resident(expand)]
                 + [_resident(w) for w in weights],
        out_specs=pl.BlockSpec((tm, D_MODEL), lambda i: (prev(i), 0)),
        out_shape=jax.ShapeDtypeStruct((S, D_MODEL), F32),
        scratch_shapes=[pltpu.VMEM((n_groups * DIL_OUT_W // LANES, tm, LANES), F32),
                        pltpu.VMEM((2 * n_groups, tm, LANES), F32),
                        pltpu.VMEM((tm, D_MODEL), F32), pltpu.VMEM((tm, D_MODEL), BF16)],
        compiler_params=pltpu.CompilerParams(
            dimension_semantics=("arbitrary",), vmem_limit_bytes=56 * MIB),
        name="post",
    )(x, yr, *os_, *ms, *ls, gates, gates, p, expand, *weights)


def _dil_constants():
    QB = DIL_BLOCK
    qi = np.arange(QB)[:, None]
    kj = np.arange(2 * QB)[None, :] - QB
    dist = qi - kj
    in_band = (dist >= 0) & (dist <= QB)
    band = np.where(np.stack([in_band & (kj >= 0), in_band]), 0.0, NEG_INF).astype(np.float32)
    expand = np.zeros((LANES, DIL_OUT_W), np.float32)
    for h in range(DIL_SLOTS):
        expand[h * LSE_LANES, h * DIL_HEAD_DIM:(h + 1) * DIL_HEAD_DIM] = 1.0
    return jnp.asarray(band), jnp.asarray(expand, BF16)


def kernel(x, p, positions, w_in, b_gate, w_ret_out, w_dil_out, w_o, g_pre_mix, g_post_mix, g_pre_mlp,
           g_post_mlp, w_up, w_down, g_pre_ple, w_ple_gate, b_ple_gate, w_ple_in, g_post_ple):
    B, S, _ = x.shape
    assert B == 1 and w_in.shape[0] == 1
    for window, dilation in DIL_GROUPS:
        assert window // dilation == DIL_BLOCK and S % (dilation * DIL_BLOCK) == 0
    xs = x[0]
    pos = positions[0].astype(F32)
    bf = lambda a: a[0].astype(BF16)
    vec = lambda a: a[0].reshape(1, -1)

    freqs = ROPE_THETA ** (-jnp.arange(0, ROPE_DIM, 2, dtype=F32) / ROPE_DIM)
    yr, a0, a1, a2, gates = _inproj(xs, pos.reshape(1, S), pos.reshape(S, 1), freqs.reshape(-1, 1),
                                    vec(g_pre_mix), bf(w_in))

    band_bias, expand = _dil_constants()
    os_, ms, ls = [], [], []
    for g, ((_, d), a) in enumerate(zip(DIL_GROUPS, (a0, a1, a2))):
        o, m, l = _dil_group(a, band_bias, g, d)
        os_.append(o)
        ms.append(m)
        ls.append(l)

    weights = (b_gate[0], bf(w_ret_out), bf(w_dil_out), bf(w_o), vec(g_post_mix), vec(g_pre_mlp), bf(w_up),
               bf(w_down), vec(g_post_mlp), vec(g_pre_ple), bf(w_ple_gate), vec(b_ple_gate), bf(w_ple_in),
               vec(g_post_ple))
    out = _post(xs, yr, os_, ms, ls, gates, p[0, 0], expand, weights)
    return out[None]
```

```python
import functools

import numpy as np
import jax
import jax.numpy as jnp
from jax import lax
from jax.experimental import pallas as pl
from jax.experimental.pallas import tpu as pltpu

F32 = jnp.float32
BF16 = jnp.bfloat16

D_MODEL = 1024
PLE_DIM = 256
EPS = 1e-6
RET_HEADS = 4
RET_QK_DIM = 256
RET_V_DIM = 512
RET_ROPE_BASE = 10000.0
DIL_GROUPS = ((128, 1), (512, 4), (2048, 16))
DIL_SLOTS = 8
DIL_HEAD_DIM = 64
DIL_BLOCK = 128
ROPE_THETA = 500000.0
ROPE_DIM = DIL_HEAD_DIM // 4
D_FF = 4 * D_MODEL
RET_QK_W = RET_HEADS * RET_QK_DIM
RET_V_W = RET_HEADS * RET_V_DIM
RET_W = 2 * RET_QK_W + 2 * RET_V_W
DIL_OUT_W = DIL_SLOTS * DIL_HEAD_DIM
DIL_W = len(DIL_GROUPS) * DIL_OUT_W
GATE_W = 2 * D_MODEL
IN_W = RET_W + DIL_W * 3 + GATE_W
OFF_DIL = RET_W
OFF_GATE = RET_W + 3 * DIL_W

LANES = 128
BF16_SUBLANES = 16
NEG_INF = -1e30
LOG2_E = 1.4426950408889634
MIB = 1024 * 1024

INPROJ_TM = 256
INPROJ_TN = 512
POST_TM = 256
DIL_NB = 8
FF_CHUNK = 1024
LSE_LANES = LANES // DIL_SLOTS


def _rms(x):
    return x * lax.rsqrt(jnp.mean(x * x, axis=-1, keepdims=True) + EPS)


def _bdot(a, b):
    return jnp.dot(a, b, preferred_element_type=F32)


def _resident(a):
    return pl.BlockSpec(a.shape, lambda *_: (0,) * a.ndim, pipeline_mode=pl.Buffered(1))


def _inproj_kernel(x_ref, pos_ref, posc_ref, f_ref, invf_ref, qdec_ref, kdec_ref, g_ref, w_ref, *rest,
                   chunk_dec, n_cast):
    cast_in, rest = rest[:n_cast], rest[n_cast:]
    yr_ref, a0_ref, a1_ref, a2_ref, gate_ref = rest[:5]
    cast_out, rest = rest[5:5 + n_cast], rest[5 + n_cast:]
    u_ref, tab_ref, scr_ref, ret_ref, r_ref = rest
    tm = x_ref.shape[0]
    tn = INPROJ_TN
    half = RET_QK_DIM // 2
    hr = ROPE_DIM // 2

    @pl.when(pl.program_id(0) == 0)
    def _():
        r_ref[...] = jnp.zeros_like(r_ref)

    u_ref[...] = (_rms(x_ref[...]) * g_ref[...]).astype(BF16)

    def proj(col, width=tn):
        return _bdot(u_ref[...], w_ref[:, col:col + width])

    for j in range(RET_V_W // tn):
        col = 2 * RET_QK_W + j * tn
        ret_ref[:, col:col + tn] = proj(col).astype(BF16)

    ang_r = posc_ref[...] * invf_ref[...]
    tab_ref[0] = jnp.cos(ang_r)
    tab_ref[1] = jnp.sin(ang_r)
    ang = f_ref[...] * pos_ref[...]
    c8, s8 = jnp.cos(ang), jnp.sin(ang)
    rest = DIL_HEAD_DIM - ROPE_DIM
    reps = LANES // DIL_HEAD_DIM
    one, zr, z8 = jnp.ones((rest, tm), F32), jnp.zeros((rest, tm), F32), jnp.zeros((hr, tm), F32)
    tab_ref[2] = jnp.concatenate([c8, c8, one] * reps, axis=0).T
    tab_ref[3] = jnp.concatenate([z8, s8, zr] * reps, axis=0).T
    tab_ref[4] = jnp.concatenate([-s8, z8, zr] * reps, axis=0).T

    for j in range(RET_V_W // tn):
        col = 2 * RET_QK_W + RET_V_W + j * tn
        y = proj(col)
        ret_ref[:, col:col + tn] = (y * jax.nn.sigmoid(y)).astype(BF16)
    for j in range(GATE_W // tn):
        gate_ref[:, j * tn:(j + 1) * tn] = proj(OFF_GATE + j * tn).astype(BF16)

    for src, dst in zip(cast_in, cast_out):
        dst[...] = src[...].astype(BF16)

    def rope(y, scale):
        cols = []
        for c in range(y.shape[1] // LANES):
            v = y[:, c * LANES:(c + 1) * LANES]
            r = v * tab_ref[2] + pltpu.roll(v, hr, 1) * tab_ref[3] + pltpu.roll(v, LANES - hr, 1) * tab_ref[4]
            cols.append(r * scale if scale != 1.0 else r)
        return jnp.concatenate(cols, axis=-1)

    heads_per_chunk = tn // RET_QK_DIM
    for sec, dec_ref in ((0, qdec_ref), (1, kdec_ref)):
        for j in range(RET_QK_W // tn):
            col = sec * RET_QK_W + j * tn
            y = proj(col)
            outs = []
            for hh in range(heads_per_chunk):
                y1 = y[:, hh * RET_QK_DIM:hh * RET_QK_DIM + half]
                y2 = y[:, hh * RET_QK_DIM + half:(hh + 1) * RET_QK_DIM]
                dec = dec_ref[j * heads_per_chunk + hh]
                cos_r, sin_r = tab_ref[0], tab_ref[1]
                outs += [(y1 * cos_r - y2 * sin_r) * dec, (y2 * cos_r + y1 * sin_r) * dec]
            ret_ref[:, col:col + tn] = jnp.concatenate(outs, axis=-1).astype(BF16)
    a_refs = (a0_ref, a1_ref, a2_ref)

    def dil_section(which, g):
        d = DIL_GROUPS[g][1]
        y = proj(OFF_DIL + which * DIL_W + g * DIL_OUT_W, DIL_OUT_W)
        if which == 0:
            y = rope(y, LOG2_E * DIL_HEAD_DIM ** -0.5)
        elif which == 1:
            y = rope(y, 1.0)
        if d == 1:
            a_refs[g][which, 0] = y.astype(BF16)
        else:
            ncol = DIL_OUT_W // LANES
            s0 = (which * len(DIL_GROUPS) + g) * ncol
            for cc in range(ncol):
                scr_ref[s0 + cc] = y[:, cc * LANES:(cc + 1) * LANES]
            for c in range(d):
                a_refs[g][which, c] = jnp.concatenate(
                    [scr_ref[s0 + cc, pl.ds(c, tm // d, stride=d), :] for cc in range(ncol)],
                    axis=-1).astype(BF16)

    causal = lax.broadcasted_iota(jnp.int32, (tm, tm), 0) >= lax.broadcasted_iota(jnp.int32, (tm, tm), 1)

    def retention_head(h):
        q = ret_ref[:, h * RET_QK_DIM:(h + 1) * RET_QK_DIM]
        k = ret_ref[:, RET_QK_W + h * RET_QK_DIM:RET_QK_W + (h + 1) * RET_QK_DIM]
        vo = 2 * RET_QK_W + h * RET_V_DIM
        go = vo + RET_V_W
        v = ret_ref[:, vo:vo + RET_V_DIM]
        s = lax.dot_general(q, k, (((1,), (1,)), ((), ())), preferred_element_type=F32)
        s = jnp.where(causal, s, 0.0).astype(BF16)
        r = r_ref[h]
        y = _bdot(s, v) + _bdot(q, r.astype(BF16))
        upd = lax.dot_general(k, v, (((0,), (0,)), ((), ())), preferred_element_type=F32)
        r_ref[h] = (r + upd) * chunk_dec[h]
        yr_ref[:, h * RET_V_DIM:(h + 1) * RET_V_DIM] = (
            _rms(y) * ret_ref[:, go:go + RET_V_DIM].astype(F32)).astype(BF16)

    sections = [(which, g) for which in range(3) for g in range(len(DIL_GROUPS))]
    per_head = -(-len(sections) // RET_HEADS)
    for h in range(RET_HEADS):
        retention_head(h)
        for which, g in sections[h * per_head:(h + 1) * per_head]:
            dil_section(which, g)


def _ret_decay():
    return np.log1p(-(2.0 ** (-5.0 - np.arange(RET_HEADS, dtype=np.float64))))


def _inproj(x, pos_row, pos_col, freqs_col, g, w, cast_ws):
    S = x.shape[0]
    tm = INPROJ_TM
    half = RET_QK_DIM // 2
    idx = np.arange(tm, dtype=np.float64)
    log_gamma = _ret_decay()
    chunk_dec = tuple(float(v) for v in np.exp(tm * log_gamma))
    qdec = np.exp((idx + 1.0)[None, :] * log_gamma[:, None])
    kdec = np.exp(-(idx + 1.0)[None, :] * log_gamma[:, None]) * RET_QK_DIM ** -0.5
    widen = lambda t: jnp.asarray(np.broadcast_to(t[:, :, None], (RET_HEADS, tm, half)), F32)
    qdec, kdec = widen(qdec), widen(kdec)
    inv_freq = (1.0 / (RET_ROPE_BASE ** jnp.linspace(0.0, 1.0, half, dtype=F32))).reshape(1, half)
    a_shapes, a_specs = [], []
    for _, d in DIL_GROUPS:
        a_shapes.append(jax.ShapeDtypeStruct((3, d, S // d, DIL_OUT_W), BF16))
        a_specs.append(pl.BlockSpec((3, d, tm // d, DIL_OUT_W), lambda i: (0, 0, i, 0)))
    nt = S // tm
    row = lambda width: pl.BlockSpec((tm, width), lambda i: (i, 0))
    cast_specs, cast_shapes = [], []
    for cw in cast_ws:
        rows = max(BF16_SUBLANES, cw.shape[0] // nt)
        steps_per_slab = nt // (cw.shape[0] // rows)
        cast_specs.append(pl.BlockSpec((rows, cw.shape[1]), lambda i, k=steps_per_slab: (i // k, 0)))
        cast_shapes.append(jax.ShapeDtypeStruct(cw.shape, BF16))
    outs = pl.pallas_call(
        functools.partial(_inproj_kernel, chunk_dec=chunk_dec, n_cast=len(cast_ws)),
        grid=(nt,),
        in_specs=[row(D_MODEL), pl.BlockSpec((1, tm), lambda i: (0, i)), row(1),
                  _resident(freqs_col), _resident(inv_freq), _resident(qdec), _resident(kdec),
                  _resident(g), _resident(w)] + cast_specs,
        out_specs=[row(RET_V_W)] + a_specs + [row(GATE_W)] + cast_specs,
        out_shape=[jax.ShapeDtypeStruct((S, RET_V_W), BF16)] + a_shapes
                  + [jax.ShapeDtypeStruct((S, GATE_W), BF16)] + cast_shapes,
        scratch_shapes=[pltpu.VMEM((tm, D_MODEL), BF16), pltpu.VMEM((5, tm, LANES), F32),
                        pltpu.VMEM((3 * DIL_W // LANES, tm, LANES), F32),
                        pltpu.VMEM((tm, RET_W), BF16), pltpu.VMEM((RET_HEADS, RET_QK_DIM, RET_V_DIM), F32)],
        compiler_params=pltpu.CompilerParams(
            dimension_semantics=("arbitrary",), vmem_limit_bytes=56 * MIB),
        name="inproj",
    )(x, pos_row, pos_col, freqs_col, inv_freq, qdec, kdec, g, w, *cast_ws)
    return outs[:5], outs[5:]


def _dil_kernel(band_ref, q_ref, k_ref, v_ref, kprev_ref, vprev_ref, o_ref, m_ref, l_ref):
    n = pl.program_id(1)
    QB = DIL_BLOCK
    lane = lax.broadcasted_iota(jnp.int32, (QB, LANES), 1)
    for b in range(q_ref.shape[0] // QB):
        rows = slice(b * QB, (b + 1) * QB)
        bias = band_ref[jnp.minimum(n, 1)] if b == 0 else band_ref[1]
        m_tile = jnp.zeros((QB, LANES), F32)
        l_tile = jnp.zeros((QB, LANES), F32)
        for pr in range(DIL_OUT_W // LANES):
            sl = slice(pr * LANES, (pr + 1) * LANES)
            q = q_ref[rows, sl]
            if b == 0:
                kp = jnp.concatenate([kprev_ref[:, sl], k_ref[rows, sl]], axis=0)
                vp = jnp.concatenate([vprev_ref[:, sl], v_ref[rows, sl]], axis=0)
            else:
                kp = k_ref[(b - 1) * QB:(b + 1) * QB, sl]
                vp = v_ref[(b - 1) * QB:(b + 1) * QB, sl]
            zero = jnp.zeros_like(q)
            qs = jnp.concatenate([jnp.where(lane < DIL_HEAD_DIM, q, zero),
                                  jnp.where(lane >= DIL_HEAD_DIM, q, zero)], axis=0)
            s = lax.dot_general(qs, kp, (((1,), (1,)), ((), ())), preferred_element_type=F32)
            ps = []
            for hh in range(2):
                sh = s[hh * QB:(hh + 1) * QB] + bias
                m = jnp.max(sh, axis=-1, keepdims=True)
                p = jnp.exp2(sh - m)
                l = jnp.sum(p, axis=-1, keepdims=True)
                ps.append(p.astype(BF16))
                head = 2 * pr + hh
                in_slot = (lane >= head * LSE_LANES) & (lane < (head + 1) * LSE_LANES)
                m_tile = jnp.where(in_slot, m, m_tile)
                l_tile = jnp.where(in_slot, l, l_tile)
            o2 = _bdot(jnp.concatenate(ps, axis=0), vp)
            o_ref[rows, sl] = jnp.where(lane < DIL_HEAD_DIM, o2[:QB], o2[QB:]).astype(BF16)
        m_ref[rows, :] = m_tile
        l_ref[rows, :] = l_tile


def _dil_group(a, band_bias, g, d):
    L = a.shape[2]
    QB = DIL_BLOCK
    W = DIL_OUT_W
    nb = DIL_NB
    T = nb * QB
    sec = lambda which: pl.BlockSpec((None, None, T, W), lambda c, n, which=which: (which, c, n, 0))
    halo = lambda which: pl.BlockSpec(
        (None, None, QB, W), lambda c, n, which=which: (which, c, jnp.maximum(n * nb - 1, 0), 0))
    return pl.pallas_call(
        _dil_kernel,
        grid=(d, L // T),
        in_specs=[_resident(band_bias), sec(0), sec(1), sec(2), halo(1), halo(2)],
        out_specs=[pl.BlockSpec((None, T, W), lambda c, n: (c, n, 0)),
                   pl.BlockSpec((None, T, LANES), lambda c, n: (c, n, 0)),
                   pl.BlockSpec((None, T, LANES), lambda c, n: (c, n, 0))],
        out_shape=[jax.ShapeDtypeStruct((d, L, W), BF16),
                   jax.ShapeDtypeStruct((d, L, LANES), F32),
                   jax.ShapeDtypeStruct((d, L, LANES), F32)],
        compiler_params=pltpu.CompilerParams(
            dimension_semantics=("parallel", "parallel"), vmem_limit_bytes=32 * MIB),
        name=f"dil_attn_{g}",
    )(band_bias, a, a, a, a, a)


def _post_kernel(x_ref, yr_ref, o1_ref, o2_ref, o3_ref, m1_ref, m2_ref, m3_ref, l1_ref, l2_ref, l3_ref,
                 gr_ref, ga_ref, p_ref, e_ref,
                 bg_ref, wro_ref, wdo_ref, wo_ref, gpm_ref, gprm_ref, wup_ref, wdn_ref, gpom_ref,
                 gprp_ref, wpg_ref, bpg_ref, wpi_ref, gpop_ref, out_ref, osc_ref, lsc_ref, h_ref, v2_ref):
    tm = x_ref.shape[0]

    @pl.when(pl.program_id(0) == 0)
    def _():
        h_ref[...] = jnp.zeros_like(h_ref)
        v2_ref[...] = jnp.zeros_like(v2_ref)

    def mlp_chunk(c, f):
        a = jnp.maximum(_bdot(v2_ref[...], wup_ref[:, c * FF_CHUNK:(c + 1) * FF_CHUNK]), 0.0)
        return f + _bdot((a * a).astype(BF16), wdn_ref[c * FF_CHUNK:(c + 1) * FF_CHUNK, :])

    f = mlp_chunk(0, jnp.zeros((tm, D_MODEL), F32))

    def interleaved(ref, scr, slot):
        d, _, width = ref.shape
        if d == 1:
            return ref[0].astype(F32)
        ncol = width // LANES
        for c in range(d):
            v = ref[c].astype(F32)
            for cc in range(ncol):
                scr[slot * ncol + cc, pl.ds(c, tm // d, stride=d), :] = v[:, cc * LANES:(cc + 1) * LANES]
        return jnp.concatenate([scr[slot * ncol + cc] for cc in range(ncol)], axis=-1)

    n_groups = len(DIL_GROUPS)
    os_ = [interleaved(r, osc_ref, i) for i, r in enumerate((o1_ref, o2_ref, o3_ref))]
    m1, m2, m3 = [interleaved(r, lsc_ref, i) for i, r in enumerate((m1_ref, m2_ref, m3_ref))]
    l1, l2, l3 = [interleaved(r, lsc_ref, n_groups + i) for i, r in enumerate((l1_ref, l2_ref, l3_ref))]

    mx = jnp.maximum(jnp.maximum(m1, m2), m3)
    e1, e2, e3 = jnp.exp2(m1 - mx), jnp.exp2(m2 - mx), jnp.exp2(m3 - mx)
    den = e1 * l1 + e2 * l2 + e3 * l3
    expand = e_ref[...]

    def widen(w):
        return _bdot(w.astype(BF16), expand)

    merged = (widen(e1 / den) * os_[0] + widen(e2 / den) * os_[1] + widen(e3 / den) * os_[2]).astype(BF16)
    f = mlp_chunk(1, f)
    sig_r = jax.nn.sigmoid(gr_ref[...].astype(F32) + bg_ref[0:1, :])
    sig_a = jax.nn.sigmoid(ga_ref[...].astype(F32) + bg_ref[1:2, :])
    f = mlp_chunk(2, f)
    mixed = (sig_r * _bdot(yr_ref[...], wro_ref[...]) + sig_a * _bdot(merged, wdo_ref[...])).astype(BF16)
    f = mlp_chunk(3, f)
    e = _bdot(p_ref[...].astype(BF16), wpi_ref[...])
    t = _bdot(mixed, wo_ref[...])
    h = h_ref[...] + _rms(f) * gpom_ref[...]
    hg = (_rms(h) * gprp_ref[...]).astype(BF16)
    gate_pre = _bdot(hg, wpg_ref[...])
    h1 = x_ref[...] + _rms(t) * gpm_ref[...]
    h_ref[...] = h1
    v2_ref[...] = (_rms(h1) * gprm_ref[...]).astype(BF16)
    out_ref[...] = h + _rms(jax.nn.sigmoid(gate_pre + bpg_ref[...]) * e) * gpop_ref[...]


def _post(x, yr, os_, ms, ls, gates, p, expand, weights):
    S = x.shape[0]
    tm = POST_TM
    nt = S // tm
    cur = lambda i: jnp.minimum(i, nt - 1)
    prev = lambda i: jnp.maximum(i - 1, 0)
    row = lambda w, j=0: pl.BlockSpec((tm, w), lambda i, j=j: (cur(i), j))
    stream = lambda a: pl.BlockSpec((a.shape[0], tm // a.shape[0], a.shape[2]), lambda i: (0, cur(i), 0))
    n_groups = len(DIL_GROUPS)
    return pl.pallas_call(
        _post_kernel,
        grid=(nt + 1,),
        in_specs=[row(D_MODEL), row(RET_V_W)] + [stream(a) for a in (*os_, *ms, *ls)]
                 + [row(D_MODEL, 0), row(D_MODEL, 1),
                    pl.BlockSpec((tm, PLE_DIM), lambda i: (prev(i), 0)), _resident(expand)]
                 + [_resident(w) for w in weights],
        out_specs=pl.BlockSpec((tm, D_MODEL), lambda i: (prev(i), 0)),
        out_shape=jax.ShapeDtypeStruct((S, D_MODEL), F32),
        scratch_shapes=[pltpu.VMEM((n_groups * DIL_OUT_W // LANES, tm, LANES), F32),
                        pltpu.VMEM((2 * n_groups, tm, LANES), F32),
                        pltpu.VMEM((tm, D_MODEL), F32), pltpu.VMEM((tm, D_MODEL), BF16)],
        compiler_params=pltpu.CompilerParams(
            dimension_semantics=("arbitrary",), vmem_limit_bytes=56 * MIB),
        name="post",
    )(x, yr, *os_, *ms, *ls, gates, gates, p, expand, *weights)


def _dil_constants():
    QB = DIL_BLOCK
    qi = np.arange(QB)[:, None]
    kj = np.arange(2 * QB)[None, :] - QB
    dist = qi - kj
    in_band = (dist >= 0) & (dist <= QB)
    band = np.where(np.stack([in_band & (kj >= 0), in_band]), 0.0, NEG_INF).astype(np.float32)
    expand = np.zeros((LANES, DIL_OUT_W), np.float32)
    for h in range(DIL_SLOTS):
        expand[h * LSE_LANES, h * DIL_HEAD_DIM:(h + 1) * DIL_HEAD_DIM] = 1.0
    return jnp.asarray(band), jnp.asarray(expand, BF16)


def kernel(x, p, positions, w_in, b_gate, w_ret_out, w_dil_out, w_o, g_pre_mix, g_post_mix, g_pre_mlp,
           g_post_mlp, w_up, w_down, g_pre_ple, w_ple_gate, b_ple_gate, w_ple_in, g_post_ple):
    B, S, _ = x.shape
    assert B == 1 and w_in.shape[0] == 1
    for window, dilation in DIL_GROUPS:
        assert window // dilation == DIL_BLOCK and S % (dilation * DIL_BLOCK) == 0
    xs = x[0]
    pos = positions[0].astype(F32)
    bf = lambda a: a[0].astype(BF16)
    vec = lambda a: a[0].reshape(1, -1)

    freqs = ROPE_THETA ** (-jnp.arange(0, ROPE_DIM, 2, dtype=F32) / ROPE_DIM)
    later_ws = (w_ret_out[0], w_dil_out[0], w_o[0], w_up[0], w_down[0], w_ple_gate[0], w_ple_in[0])
    (yr, a0, a1, a2, gates), later_bf = _inproj(xs, pos.reshape(1, S), pos.reshape(S, 1), freqs.reshape(-1, 1),
                                                vec(g_pre_mix), bf(w_in), later_ws)
    wro_b, wdo_b, wo_b, wup_b, wdn_b, wpg_b, wpi_b = later_bf

    band_bias, expand = _dil_constants()
    os_, ms, ls = [], [], []
    for g, ((_, d), a) in enumerate(zip(DIL_GROUPS, (a0, a1, a2))):
        o, m, l = _dil_group(a, band_bias, g, d)
        os_.append(o)
        ms.append(m)
        ls.append(l)

    weights = (b_gate[0], wro_b, wdo_b, wo_b, vec(g_post_mix), vec(g_pre_mlp), wup_b,
               wdn_b, vec(g_post_mlp), vec(g_pre_ple), wpg_b, vec(b_ple_gate), wpi_b,
               vec(g_post_ple))
    out = _post(xs, yr, os_, ms, ls, gates, p[0, 0], expand, weights)
    return out[None]
```

```python
import functools

import numpy as np
import jax
import jax.numpy as jnp
from jax import lax
from jax.experimental import pallas as pl
from jax.experimental.pallas import tpu as pltpu

F32 = jnp.float32
BF16 = jnp.bfloat16

D_MODEL = 1024
PLE_DIM = 256
EPS = 1e-6
RET_HEADS = 4
RET_QK_DIM = 256
RET_V_DIM = 512
RET_ROPE_BASE = 10000.0
DIL_GROUPS = ((128, 1), (512, 4), (2048, 16))
DIL_SLOTS = 8
DIL_HEAD_DIM = 64
DIL_BLOCK = 128
ROPE_THETA = 500000.0
ROPE_DIM = DIL_HEAD_DIM // 4
D_FF = 4 * D_MODEL
RET_QK_W = RET_HEADS * RET_QK_DIM
RET_V_W = RET_HEADS * RET_V_DIM
RET_W = 2 * RET_QK_W + 2 * RET_V_W
DIL_OUT_W = DIL_SLOTS * DIL_HEAD_DIM
DIL_W = len(DIL_GROUPS) * DIL_OUT_W
GATE_W = 2 * D_MODEL
IN_W = RET_W + DIL_W * 3 + GATE_W
OFF_DIL = RET_W
OFF_GATE = RET_W + 3 * DIL_W

LANES = 128
BF16_SUBLANES = 16
NEG_INF = -1e30
LOG2_E = 1.4426950408889634
MIB = 1024 * 1024

INPROJ_TM = 256
INPROJ_TN = 512
POST_TM = 256
DIL_NB = 8
FF_CHUNK = 1024
LSE_LANES = LANES // DIL_SLOTS


def _rms(x):
    return x * lax.rsqrt(jnp.mean(x * x, axis=-1, keepdims=True) + EPS)


def _bdot(a, b):
    return jnp.dot(a, b, preferred_element_type=F32)


def _resident(a):
    return pl.BlockSpec(a.shape, lambda *_: (0,) * a.ndim, pipeline_mode=pl.Buffered(1))


def _inproj_kernel(x_ref, pos_ref, f_ref, invf_ref, qdec_ref, kdec_ref, g_ref, w_ref, *rest,
                   chunk_dec, n_cast):
    cast_in, rest = rest[:n_cast], rest[n_cast:]
    yr_ref, a0_ref, a1_ref, a2_ref, gate_ref = rest[:5]
    cast_out, rest = rest[5:5 + n_cast], rest[5 + n_cast:]
    u_ref, tab_ref, scr_ref, ret_ref, r_ref = rest
    tm = x_ref.shape[0]
    tn = INPROJ_TN
    half = RET_QK_DIM // 2
    hr = ROPE_DIM // 2

    @pl.when(pl.program_id(0) == 0)
    def _():
        r_ref[...] = jnp.zeros_like(r_ref)

    u_ref[...] = (_rms(x_ref[...]) * g_ref[...]).astype(BF16)

    def proj(col, width=tn):
        return _bdot(u_ref[...], w_ref[:, col:col + width])

    for j in range(RET_V_W // tn):
        col = 2 * RET_QK_W + j * tn
        ret_ref[:, col:col + tn] = proj(col).astype(BF16)

    ang_r = invf_ref[...] * pos_ref[...]
    tab_ref[0] = jnp.cos(ang_r).T
    tab_ref[1] = jnp.sin(ang_r).T
    ang = f_ref[...] * pos_ref[...]
    c8, s8 = jnp.cos(ang), jnp.sin(ang)
    rest = DIL_HEAD_DIM - ROPE_DIM
    reps = LANES // DIL_HEAD_DIM
    one, zr, z8 = jnp.ones((rest, tm), F32), jnp.zeros((rest, tm), F32), jnp.zeros((hr, tm), F32)
    tab_ref[2] = jnp.concatenate([c8, c8, one] * reps, axis=0).T
    tab_ref[3] = jnp.concatenate([z8, s8, zr] * reps, axis=0).T
    tab_ref[4] = jnp.concatenate([-s8, z8, zr] * reps, axis=0).T

    for j in range(RET_V_W // tn):
        col = 2 * RET_QK_W + RET_V_W + j * tn
        y = proj(col)
        ret_ref[:, col:col + tn] = (y * jax.nn.sigmoid(y)).astype(BF16)
    for j in range(GATE_W // tn):
        gate_ref[:, j * tn:(j + 1) * tn] = proj(OFF_GATE + j * tn).astype(BF16)

    for src, dst in zip(cast_in, cast_out):
        dst[...] = src[...].astype(BF16)

    def rope(y, scale):
        cols = []
        for c in range(y.shape[1] // LANES):
            v = y[:, c * LANES:(c + 1) * LANES]
            r = v * tab_ref[2] + pltpu.roll(v, hr, 1) * tab_ref[3] + pltpu.roll(v, LANES - hr, 1) * tab_ref[4]
            cols.append(r * scale if scale != 1.0 else r)
        return jnp.concatenate(cols, axis=-1)

    heads_per_chunk = tn // RET_QK_DIM
    for sec, dec_ref in ((0, qdec_ref), (1, kdec_ref)):
        for j in range(RET_QK_W // tn):
            col = sec * RET_QK_W + j * tn
            y = proj(col)
            outs = []
            for hh in range(heads_per_chunk):
                y1 = y[:, hh * RET_QK_DIM:hh * RET_QK_DIM + half]
                y2 = y[:, hh * RET_QK_DIM + half:(hh + 1) * RET_QK_DIM]
                dec = dec_ref[j * heads_per_chunk + hh]
                cos_r, sin_r = tab_ref[0], tab_ref[1]
                outs += [(y1 * cos_r - y2 * sin_r) * dec, (y2 * cos_r + y1 * sin_r) * dec]
            ret_ref[:, col:col + tn] = jnp.concatenate(outs, axis=-1).astype(BF16)
    a_refs = (a0_ref, a1_ref, a2_ref)

    def dil_section(which, g):
        d = DIL_GROUPS[g][1]
        y = proj(OFF_DIL + which * DIL_W + g * DIL_OUT_W, DIL_OUT_W)
        if which == 0:
            y = rope(y, LOG2_E * DIL_HEAD_DIM ** -0.5)
        elif which == 1:
            y = rope(y, 1.0)
        if d == 1:
            a_refs[g][which, 0] = y.astype(BF16)
        else:
            ncol = DIL_OUT_W // LANES
            s0 = (which * len(DIL_GROUPS) + g) * ncol
            for cc in range(ncol):
                scr_ref[s0 + cc] = y[:, cc * LANES:(cc + 1) * LANES]
            for c in range(d):
                a_refs[g][which, c] = jnp.concatenate(
                    [scr_ref[s0 + cc, pl.ds(c, tm // d, stride=d), :] for cc in range(ncol)],
                    axis=-1).astype(BF16)

    causal = lax.broadcasted_iota(jnp.int32, (tm, tm), 0) >= lax.broadcasted_iota(jnp.int32, (tm, tm), 1)

    def retention_head(h):
        q = ret_ref[:, h * RET_QK_DIM:(h + 1) * RET_QK_DIM]
        k = ret_ref[:, RET_QK_W + h * RET_QK_DIM:RET_QK_W + (h + 1) * RET_QK_DIM]
        vo = 2 * RET_QK_W + h * RET_V_DIM
        go = vo + RET_V_W
        v = ret_ref[:, vo:vo + RET_V_DIM]
        s = lax.dot_general(q, k, (((1,), (1,)), ((), ())), preferred_element_type=F32)
        s = jnp.where(causal, s, 0.0).astype(BF16)
        r = r_ref[h]
        y = _bdot(s, v) + _bdot(q, r.astype(BF16))
        upd = lax.dot_general(k, v, (((0,), (0,)), ((), ())), preferred_element_type=F32)
        r_ref[h] = (r + upd) * chunk_dec[h]
        yr_ref[:, h * RET_V_DIM:(h + 1) * RET_V_DIM] = (
            _rms(y) * ret_ref[:, go:go + RET_V_DIM].astype(F32)).astype(BF16)

    sections = [(which, g) for which in range(3) for g in range(len(DIL_GROUPS))]
    per_head = -(-len(sections) // RET_HEADS)
    for h in range(RET_HEADS):
        retention_head(h)
        for which, g in sections[h * per_head:(h + 1) * per_head]:
            dil_section(which, g)


def _ret_decay():
    return np.log1p(-(2.0 ** (-5.0 - np.arange(RET_HEADS, dtype=np.float64))))


def _inproj(x, pos_row, freqs_col, g, w, cast_ws):
    S = x.shape[0]
    tm = INPROJ_TM
    half = RET_QK_DIM // 2
    idx = np.arange(tm, dtype=np.float64)
    log_gamma = _ret_decay()
    chunk_dec = tuple(float(v) for v in np.exp(tm * log_gamma))
    qdec = np.exp((idx + 1.0)[None, :] * log_gamma[:, None])
    kdec = np.exp(-(idx + 1.0)[None, :] * log_gamma[:, None]) * RET_QK_DIM ** -0.5
    widen = lambda t: jnp.asarray(np.broadcast_to(t[:, :, None], (RET_HEADS, tm, half)), F32)
    qdec, kdec = widen(qdec), widen(kdec)
    inv_freq = (1.0 / (RET_ROPE_BASE ** jnp.linspace(0.0, 1.0, half, dtype=F32))).reshape(half, 1)
    a_shapes, a_specs = [], []
    for _, d in DIL_GROUPS:
        a_shapes.append(jax.ShapeDtypeStruct((3, d, S // d, DIL_OUT_W), BF16))
        a_specs.append(pl.BlockSpec((3, d, tm // d, DIL_OUT_W), lambda i: (0, 0, i, 0)))
    nt = S // tm
    row = lambda width: pl.BlockSpec((tm, width), lambda i: (i, 0))
    cast_specs, cast_shapes = [], []
    for cw in cast_ws:
        rows = max(BF16_SUBLANES, cw.shape[0] // nt)
        steps_per_slab = nt // (cw.shape[0] // rows)
        cast_specs.append(pl.BlockSpec((rows, cw.shape[1]), lambda i, k=steps_per_slab: (i // k, 0)))
        cast_shapes.append(jax.ShapeDtypeStruct(cw.shape, BF16))
    outs = pl.pallas_call(
        functools.partial(_inproj_kernel, chunk_dec=chunk_dec, n_cast=len(cast_ws)),
        grid=(nt,),
        in_specs=[row(D_MODEL), pl.BlockSpec((1, tm), lambda i: (0, i)),
                  _resident(freqs_col), _resident(inv_freq), _resident(qdec), _resident(kdec),
                  _resident(g), _resident(w)] + cast_specs,
        out_specs=[row(RET_V_W)] + a_specs + [row(GATE_W)] + cast_specs,
        out_shape=[jax.ShapeDtypeStruct((S, RET_V_W), BF16)] + a_shapes
                  + [jax.ShapeDtypeStruct((S, GATE_W), BF16)] + cast_shapes,
        scratch_shapes=[pltpu.VMEM((tm, D_MODEL), BF16), pltpu.VMEM((5, tm, LANES), F32),
                        pltpu.VMEM((3 * DIL_W // LANES, tm, LANES), F32),
                        pltpu.VMEM((tm, RET_W), BF16), pltpu.VMEM((RET_HEADS, RET_QK_DIM, RET_V_DIM), F32)],
        compiler_params=pltpu.CompilerParams(
            dimension_semantics=("arbitrary",), vmem_limit_bytes=56 * MIB),
        name="inproj",
    )(x, pos_row, freqs_col, inv_freq, qdec, kdec, g, w, *cast_ws)
    return outs[:5], outs[5:]


def _dil_kernel(band_ref, q_ref, k_ref, v_ref, kprev_ref, vprev_ref, o_ref, st_ref):
    n = pl.program_id(1)
    QB = DIL_BLOCK
    lane = lax.broadcasted_iota(jnp.int32, (QB, LANES), 1)
    for b in range(q_ref.shape[0] // QB):
        rows = slice(b * QB, (b + 1) * QB)
        bias_t = band_ref[jnp.minimum(n, 1)] if b == 0 else band_ref[1]
        eye2 = (lax.broadcasted_iota(jnp.int32, (2 * QB, QB), 0) % QB
                == lax.broadcasted_iota(jnp.int32, (2 * QB, QB), 1)).astype(BF16)
        for pr in range(DIL_OUT_W // LANES):
            sl = slice(pr * LANES, (pr + 1) * LANES)
            q = q_ref[rows, sl]
            if b == 0:
                kp = jnp.concatenate([kprev_ref[:, sl], k_ref[rows, sl]], axis=0)
                vp = jnp.concatenate([vprev_ref[:, sl], v_ref[rows, sl]], axis=0)
            else:
                kp = k_ref[(b - 1) * QB:(b + 1) * QB, sl]
                vp = v_ref[(b - 1) * QB:(b + 1) * QB, sl]
            zero = jnp.zeros_like(q)
            qs = jnp.concatenate([jnp.where(lane < DIL_HEAD_DIM, q, zero),
                                  jnp.where(lane >= DIL_HEAD_DIM, q, zero)], axis=0)
            s = lax.dot_general(jnp.concatenate([qs, eye2], axis=1), jnp.concatenate([kp, bias_t], axis=1),
                                (((1,), (1,)), ((), ())), preferred_element_type=F32)
            ps = []
            for hh in range(2):
                sh = s[hh * QB:(hh + 1) * QB]
                m = jnp.max(sh, axis=-1, keepdims=True)
                p = jnp.exp2(sh - m)
                l = jnp.sum(p, axis=-1, keepdims=True)
                ps.append(p.astype(BF16))
                s0 = (2 * pr + hh) * LSE_LANES
                st_ref[rows, s0:s0 + LSE_LANES] = jnp.broadcast_to(m, (QB, LSE_LANES))
                st_ref[rows, LANES + s0:LANES + s0 + LSE_LANES] = jnp.broadcast_to(l, (QB, LSE_LANES))
            o2 = _bdot(jnp.concatenate(ps, axis=0), vp)
            o_ref[rows, sl] = jnp.where(lane < DIL_HEAD_DIM, o2[:QB], o2[QB:]).astype(BF16)


def _dil_group(a, band_bias, g, d):
    L = a.shape[2]
    QB = DIL_BLOCK
    W = DIL_OUT_W
    nb = DIL_NB
    T = nb * QB
    sec = lambda which: pl.BlockSpec((None, None, T, W), lambda c, n, which=which: (which, c, n, 0))
    halo = lambda which: pl.BlockSpec(
        (None, None, QB, W), lambda c, n, which=which: (which, c, jnp.maximum(n * nb - 1, 0), 0))
    return pl.pallas_call(
        _dil_kernel,
        grid=(d, L // T),
        in_specs=[_resident(band_bias), sec(0), sec(1), sec(2), halo(1), halo(2)],
        out_specs=[pl.BlockSpec((None, T, W), lambda c, n: (c, n, 0)),
                   pl.BlockSpec((None, T, 2 * LANES), lambda c, n: (c, n, 0))],
        out_shape=[jax.ShapeDtypeStruct((d, L, W), BF16),
                   jax.ShapeDtypeStruct((d, L, 2 * LANES), F32)],
        compiler_params=pltpu.CompilerParams(
            dimension_semantics=("parallel", "parallel"), vmem_limit_bytes=32 * MIB),
        name=f"dil_attn_{g}",
    )(band_bias, a, a, a, a, a)


def _post_kernel(x_ref, yr_ref, o1_ref, o2_ref, o3_ref, st1_ref, st2_ref, st3_ref,
                 gr_ref, ga_ref, p_ref, e_ref,
                 bg_ref, wro_ref, wdo_ref, wo_ref, gpm_ref, gprm_ref, wup_ref, wdn_ref, gpom_ref,
                 gprp_ref, wpg_ref, bpg_ref, wpi_ref, gpop_ref, out_ref, osc_ref, lsc_ref, h_ref, v2_ref):
    tm = x_ref.shape[0]

    @pl.when(pl.program_id(0) == 0)
    def _():
        h_ref[...] = jnp.zeros_like(h_ref)
        v2_ref[...] = jnp.zeros_like(v2_ref)

    def mlp_chunk(c, f):
        a = jnp.maximum(_bdot(v2_ref[...], wup_ref[:, c * FF_CHUNK:(c + 1) * FF_CHUNK]), 0.0)
        return f + _bdot((a * a).astype(BF16), wdn_ref[c * FF_CHUNK:(c + 1) * FF_CHUNK, :])

    f = mlp_chunk(0, jnp.zeros((tm, D_MODEL), F32))

    def interleaved(ref, scr, slot):
        d, _, width = ref.shape
        if d == 1:
            return ref[0].astype(F32)
        ncol = width // LANES
        for c in range(d):
            v = ref[c].astype(F32)
            for cc in range(ncol):
                scr[slot * ncol + cc, pl.ds(c, tm // d, stride=d), :] = v[:, cc * LANES:(cc + 1) * LANES]
        return jnp.concatenate([scr[slot * ncol + cc] for cc in range(ncol)], axis=-1)

    n_groups = len(DIL_GROUPS)
    os_ = [interleaved(r, osc_ref, i) for i, r in enumerate((o1_ref, o2_ref, o3_ref))]
    stats = [interleaved(r, lsc_ref, i) for i, r in enumerate((st1_ref, st2_ref, st3_ref))]
    m1, m2, m3 = [s[:, :LANES] for s in stats]
    l1, l2, l3 = [s[:, LANES:] for s in stats]

    mx = jnp.maximum(jnp.maximum(m1, m2), m3)
    e1, e2, e3 = jnp.exp2(m1 - mx), jnp.exp2(m2 - mx), jnp.exp2(m3 - mx)
    den = e1 * l1 + e2 * l2 + e3 * l3
    expand = e_ref[...]

    def widen(w):
        return _bdot(w.astype(BF16), expand)

    merged = (widen(e1 / den) * os_[0] + widen(e2 / den) * os_[1] + widen(e3 / den) * os_[2]).astype(BF16)
    f = mlp_chunk(1, f)
    sig_r = jax.nn.sigmoid(gr_ref[...].astype(F32) + bg_ref[0:1, :])
    sig_a = jax.nn.sigmoid(ga_ref[...].astype(F32) + bg_ref[1:2, :])
    f = mlp_chunk(2, f)
    mixed = (sig_r * _bdot(yr_ref[...], wro_ref[...]) + sig_a * _bdot(merged, wdo_ref[...])).astype(BF16)
    f = mlp_chunk(3, f)
    e = _bdot(p_ref[...].astype(BF16), wpi_ref[...])
    t = _bdot(mixed, wo_ref[...])
    h = h_ref[...] + _rms(f) * gpom_ref[...]
    hg = (_rms(h) * gprp_ref[...]).astype(BF16)
    gate_pre = _bdot(hg, wpg_ref[...])
    h1 = x_ref[...] + _rms(t) * gpm_ref[...]
    h_ref[...] = h1
    v2_ref[...] = (_rms(h1) * gprm_ref[...]).astype(BF16)
    out_ref[...] = h + _rms(jax.nn.sigmoid(gate_pre + bpg_ref[...]) * e) * gpop_ref[...]


def _post(x, yr, os_, sts, gates, p, expand, weights):
    S = x.shape[0]
    tm = POST_TM
    nt = S // tm
    cur = lambda i: jnp.minimum(i, nt - 1)
    prev = lambda i: jnp.maximum(i - 1, 0)
    row = lambda w, j=0: pl.BlockSpec((tm, w), lambda i, j=j: (cur(i), j))
    stream = lambda a: pl.BlockSpec((a.shape[0], tm // a.shape[0], a.shape[2]), lambda i: (0, cur(i), 0))
    n_groups = len(DIL_GROUPS)
    return pl.pallas_call(
        _post_kernel,
        grid=(nt + 1,),
        in_specs=[row(D_MODEL), row(RET_V_W)] + [stream(a) for a in (*os_, *sts)]
                 + [row(D_MODEL, 0), row(D_MODEL, 1),
                    pl.BlockSpec((tm, PLE_DIM), lambda i: (prev(i), 0)), _resident(expand)]
                 + [_resident(w) for w in weights],
        out_specs=pl.BlockSpec((tm, D_MODEL), lambda i: (prev(i), 0)),
        out_shape=jax.ShapeDtypeStruct((S, D_MODEL), F32),
        scratch_shapes=[pltpu.VMEM((n_groups * DIL_OUT_W // LANES, tm, LANES), F32),
                        pltpu.VMEM((2 * n_groups, tm, LANES), F32),
                        pltpu.VMEM((tm, D_MODEL), F32), pltpu.VMEM((tm, D_MODEL), BF16)],
        compiler_params=pltpu.CompilerParams(
            dimension_semantics=("arbitrary",), vmem_limit_bytes=56 * MIB),
        name="post",
    )(x, yr, *os_, *sts, gates, gates, p, expand, *weights)


def _dil_constants():
    QB = DIL_BLOCK
    qi = np.arange(QB)[:, None]
    kj = np.arange(2 * QB)[None, :] - QB
    dist = qi - kj
    in_band = (dist >= 0) & (dist <= QB)
    band = np.where(np.stack([in_band & (kj >= 0), in_band]), 0.0, NEG_INF).astype(np.float32)
    band = np.transpose(band, (0, 2, 1))
    expand = np.zeros((LANES, DIL_OUT_W), np.float32)
    for h in range(DIL_SLOTS):
        expand[h * LSE_LANES, h * DIL_HEAD_DIM:(h + 1) * DIL_HEAD_DIM] = 1.0
    return jnp.asarray(band, BF16), jnp.asarray(expand, BF16)


def kernel(x, p, positions, w_in, b_gate, w_ret_out, w_dil_out, w_o, g_pre_mix, g_post_mix, g_pre_mlp,
           g_post_mlp, w_up, w_down, g_pre_ple, w_ple_gate, b_ple_gate, w_ple_in, g_post_ple):
    B, S, _ = x.shape
    assert B == 1 and w_in.shape[0] == 1
    for window, dilation in DIL_GROUPS:
        assert window // dilation == DIL_BLOCK and S % (dilation * DIL_BLOCK) == 0
    xs = x[0]
    pos = positions[0].astype(F32)
    bf = lambda a: a[0].astype(BF16)
    vec = lambda a: a[0].reshape(1, -1)

    freqs = ROPE_THETA ** (-jnp.arange(0, ROPE_DIM, 2, dtype=F32) / ROPE_DIM)
    later_ws = (w_ret_out[0], w_dil_out[0], w_o[0], w_up[0], w_down[0], w_ple_gate[0], w_ple_in[0])
    (yr, a0, a1, a2, gates), later_bf = _inproj(xs, pos.reshape(1, S), freqs.reshape(-1, 1),
                                                vec(g_pre_mix), bf(w_in), later_ws)
    wro_b, wdo_b, wo_b, wup_b, wdn_b, wpg_b, wpi_b = later_bf

    band_bias, expand = _dil_constants()
    os_, sts = [], []
    for g, ((_, d), a) in enumerate(zip(DIL_GROUPS, (a0, a1, a2))):
        o, st = _dil_group(a, band_bias, g, d)
        os_.append(o)
        sts.append(st)

    weights = (b_gate[0], wro_b, wdo_b, wo_b, vec(g_post_mix), vec(g_pre_mlp), wup_b,
               wdn_b, vec(g_post_mlp), vec(g_pre_ple), wpg_b, vec(b_ple_gate), wpi_b,
               vec(g_post_ple))
    out = _post(xs, yr, os_, sts, gates, p[0, 0], expand, weights)
    return out[None]
```
